```python
import math
import jax, jax.numpy as jnp
from jax import lax
import numpy as np

D_MODEL = 1024
BATCH = 8
SEQ = 4096
DEPTH = 1

D_MIX = 2 * D_MODEL
ATTN_WIDTH = D_MIX // 2
SB_HEAD_DIM = 64
SB_HEADS = ATTN_WIDTH // SB_HEAD_DIM
SB_BLOCK = 128
SSD_WIDTH = D_MIX - ATTN_WIDTH
SSD_HEAD_DIM = 64
SSD_HEADS = SSD_WIDTH // SSD_HEAD_DIM
SSD_GROUPS = 2
SSD_STATE = 128
SSD_CONV_K = 4
SSD_CHUNK = 128
SSD_CONV_CH = SSD_WIDTH + 2 * SSD_GROUPS * SSD_STATE
IN_SPLITS = (ATTN_WIDTH, ATTN_WIDTH, ATTN_WIDTH, SSD_WIDTH, SSD_WIDTH,
             SSD_GROUPS * SSD_STATE, SSD_GROUPS * SSD_STATE, SSD_HEADS)
D_IN = sum(IN_SPLITS)
MEM_LEN = 256
XA_HEADS = 4
XA_HEAD_DIM = D_MODEL // XA_HEADS
D_FF = 2816
EPS = 1e-6

kernel_name = "hybrid_sb_ssd_macaron_layer"


def rmsnorm(x, g):
    x32 = x.astype(jnp.float32)
    y = x32 * lax.rsqrt(jnp.mean(x32 * x32, axis=-1, keepdims=True) + EPS)
    return (y * g.astype(jnp.float32)).astype(x.dtype)


def swiglu(h, w_gu, w_down):
    gu = h @ w_gu
    g, u = jnp.split(gu, 2, axis=-1)
    return (jax.nn.silu(g) * u) @ w_down


def stick_breaking_attention(q, k, v):
    b_, h_, s_, d_ = q.shape
    scale = 1.0 / math.sqrt(d_)
    outs = []
    for i in range(s_ // SB_BLOCK):
        t0, t1 = i * SB_BLOCK, (i + 1) * SB_BLOCK
        qb = q[:, :, t0:t1].astype(jnp.float32)
        kb = k[:, :, :t1].astype(jnp.float32)
        vb = v[:, :, :t1]
        z = jnp.einsum('bhtd,bhsd->bhts', qb, kb) * scale
        t_idx = t0 + jnp.arange(SB_BLOCK)
        s_idx = jnp.arange(t1)
        causal = s_idx[None, :] < t_idx[:, None]
        log_1mb = jnp.where(causal, jax.nn.log_sigmoid(-z), 0.0)
        suffix = lax.cumsum(log_1mb, axis=3, reverse=True) - log_1mb
        log_a = jnp.where(causal, jax.nn.log_sigmoid(z) + suffix, -jnp.inf)
        a = jnp.exp(log_a)
        outs.append(jnp.einsum('bhts,bhsd->bhtd', a, vb.astype(jnp.float32)))
    return jnp.concatenate(outs, axis=2).astype(v.dtype)


def ssd_chunked(xs, dt, a, bmat, cmat):
    b_, s_, h_, p_ = xs.shape
    L, G, N = SSD_CHUNK, SSD_GROUPS, SSD_STATE
    R = h_ // G
    nc = s_ // L
    x = (xs.astype(jnp.float32) * dt[..., None]).reshape(b_, nc, L, G, R, p_)
    adt = (dt * a).reshape(b_, nc, L, G, R).transpose(0, 3, 4, 1, 2)
    a_cs = jnp.cumsum(adt, axis=-1)
    bc = bmat.astype(jnp.float32).reshape(b_, nc, L, G, N)
    cc = cmat.astype(jnp.float32).reshape(b_, nc, L, G, N)
    tri = jnp.tril(jnp.ones((L, L), dtype=bool))
    diff = a_cs[..., :, None] - a_cs[..., None, :]
    lmat = jnp.exp(jnp.where(tri, diff, -jnp.inf))
    cb = jnp.einsum('bclgn,bcsgn->bcgls', cc, bc)
    y_diag = jnp.einsum('bcgls,bgrcls,bcsgrp->bclgrp', cb, lmat, x)
    decay_states = jnp.exp(a_cs[..., -1:] - a_cs)
    states = jnp.einsum('bclgn,bgrcl,bclgrp->bcgrpn', bc, decay_states, x)
    chunk_decay = jnp.exp(a_cs[..., -1])

    def step(h, inp):
        st, dec = inp
        return h * dec[..., None, None] + st, h

    h0 = jnp.zeros_like(states[:, 0])
    _, states_in = lax.scan(step, h0, (jnp.moveaxis(states, 1, 0),
                                       jnp.moveaxis(chunk_decay, -1, 0)))
    states_in = jnp.moveaxis(states_in, 0, 1)
    y_off = jnp.einsum('bclgn,bcgrpn,bgrcl->bclgrp', cc, states_in, jnp.exp(a_cs))
    return (y_diag + y_off).reshape(b_, s_, h_, p_)


def hybrid_mixer(u, w_in, conv_w, conv_b, dt_bias, a_log, d_skip,
                 ssd_norm_g, attn_norm_g, w_out):
    b_, s_, _ = u.shape
    proj = u @ w_in
    q, k, v, z, xs, bm, cm, dt_raw = jnp.split(proj, np.cumsum(IN_SPLITS)[:-1], axis=-1)
    def heads(t):
        return t.reshape(b_, s_, SB_HEADS, SB_HEAD_DIM).transpose(0, 2, 1, 3)
    o_att = stick_breaking_attention(heads(q), heads(k), heads(v))
    o_att = o_att.transpose(0, 2, 1, 3).reshape(b_, s_, ATTN_WIDTH)
    o_att = rmsnorm(o_att, attn_norm_g)
    xbc = jnp.concatenate([xs, bm, cm], axis=-1)
    xbc = lax.conv_general_dilated(
        xbc, conv_w.astype(xbc.dtype)[:, None, :], window_strides=(1,),
        padding=[(SSD_CONV_K - 1, 0)], dimension_numbers=('NWC', 'WIO', 'NWC'),
        feature_group_count=SSD_CONV_CH)
    xbc = jax.nn.silu(xbc + conv_b)
    xs, bm, cm = jnp.split(xbc, [SSD_WIDTH, SSD_WIDTH + SSD_GROUPS * SSD_STATE], axis=-1)
    xs = xs.reshape(b_, s_, SSD_HEADS, SSD_HEAD_DIM)
    bm = bm.reshape(b_, s_, SSD_GROUPS, SSD_STATE)
    cm = cm.reshape(b_, s_, SSD_GROUPS, SSD_STATE)
    dt = jax.nn.softplus(dt_raw.astype(jnp.float32) + dt_bias.astype(jnp.float32))
    a = -jnp.exp(a_log.astype(jnp.float32))
    y = ssd_chunked(xs, dt, a, bm, cm)
    y = y + d_skip.astype(jnp.float32)[:, None] * xs.astype(jnp.float32)
    y = y.reshape(b_, s_, SSD_WIDTH) * jax.nn.silu(z.astype(jnp.float32))
    y = rmsnorm(y.reshape(b_, s_, SSD_GROUPS, SSD_WIDTH // SSD_GROUPS),
                ssd_norm_g.reshape(SSD_GROUPS, SSD_WIDTH // SSD_GROUPS))
    o_ssd = y.reshape(b_, s_, SSD_WIDTH).astype(u.dtype)
    return jnp.concatenate([o_att, o_ssd], axis=-1) @ w_out


def memory_cross_attention(h, mem_n, wq, wkv, wo):
    b_, s_, _ = h.shape
    q = (h @ wq).reshape(b_, s_, XA_HEADS, XA_HEAD_DIM)
    kv = mem_n @ wkv
    k, v = jnp.split(kv, 2, axis=-1)
    k = k.reshape(b_, MEM_LEN, XA_HEADS, XA_HEAD_DIM)
    v = v.reshape(b_, MEM_LEN, XA_HEADS, XA_HEAD_DIM)
    sc = jnp.einsum('bshd,bmhd->bhsm', q.astype(jnp.float32), k.astype(jnp.float32))
    p = jax.nn.softmax(sc / math.sqrt(XA_HEAD_DIM), axis=-1)
    o = jnp.einsum('bhsm,bmhd->bshd', p, v.astype(jnp.float32)).astype(h.dtype)
    return o.reshape(b_, s_, D_MODEL) @ wo


def setup_inputs(seed: int = 0) -> dict:
    key = jax.random.key(seed)
    ks = iter(jax.random.split(key, 40))

    def w(shape, fan_in):
        return jax.random.normal(next(ks), (DEPTH,) + shape, jnp.float32) * fan_in ** -0.5

    def gain(n):
        return 1.0 + 0.02 * jax.random.normal(next(ks), (DEPTH, n), jnp.float32)

    x = jax.random.normal(next(ks), (BATCH, SEQ, D_MODEL), jnp.float32)
    mem = jax.random.normal(next(ks), (BATCH, MEM_LEN, D_MODEL), jnp.float32)
    dt0 = jnp.exp(jax.random.uniform(next(ks), (DEPTH, SSD_HEADS), jnp.float32,
                                     math.log(1e-3), math.log(1e-1)))
    dt_bias = dt0 + jnp.log(-jnp.expm1(-dt0))
    a_log = jnp.log(jax.random.uniform(next(ks), (DEPTH, SSD_HEADS), jnp.float32, 1.0, 16.0))
    return {
        "x": x, "mem": mem,
        "ffn1_pre_g": gain(D_MODEL),
        "ffn1_w_gu": w((D_MODEL, 2 * D_FF), D_MODEL),
        "ffn1_w_down": w((D_FF, D_MODEL), D_FF),
        "ffn1_post_g": gain(D_MODEL),
        "mix_pre_g": gain(D_MODEL),
        "w_in": w((D_MODEL, D_IN), D_MODEL),
        "conv_w": w((SSD_CONV_K, SSD_CONV_CH), SSD_CONV_K),
        "conv_b": 0.02 * jax.random.normal(next(ks), (DEPTH, SSD_CONV_CH), jnp.float32),
        "dt_bias": dt_bias,
        "a_log": a_log,
        "d_skip": 1.0 + 0.1 * jax.random.normal(next(ks), (DEPTH, SSD_HEADS), jnp.float32),
        "ssd_norm_g": gain(SSD_WIDTH),
        "attn_norm_g": gain(ATTN_WIDTH),
        "w_out": w((D_MIX, D_MODEL), D_MIX),
        "mix_post_g": gain(D_MODEL),
        "xa_pre_g": gain(D_MODEL),
        "mem_g": gain(D_MODEL),
        "xa_wq": w((D_MODEL, D_MODEL), D_MODEL),
        "xa_wkv": w((D_MODEL, 2 * D_MODEL), D_MODEL),
        "xa_wo": w((D_MODEL, D_MODEL), D_MODEL),
        "xa_post_g": gain(D_MODEL),
        "ffn2_pre_g": gain(D_MODEL),
        "ffn2_w_gu": w((D_MODEL, 2 * D_FF), D_MODEL),
        "ffn2_w_down": w((D_FF, D_MODEL), D_FF),
        "ffn2_post_g": gain(D_MODEL),
    }


def reference(x, mem, ffn1_pre_g, ffn1_w_gu, ffn1_w_down, ffn1_post_g,
              mix_pre_g, w_in, conv_w, conv_b, dt_bias, a_log, d_skip,
              ssd_norm_g, attn_norm_g, w_out, mix_post_g,
              xa_pre_g, mem_g, xa_wq, xa_wkv, xa_wo, xa_post_g,
              ffn2_pre_g, ffn2_w_gu, ffn2_w_down, ffn2_post_g):
    h = x
    for l in range(DEPTH):
        f = swiglu(rmsnorm(h, ffn1_pre_g[l]), ffn1_w_gu[l], ffn1_w_down[l])
        h = h + 0.5 * rmsnorm(f, ffn1_post_g[l])
        m = hybrid_mixer(rmsnorm(h, mix_pre_g[l]), w_in[l], conv_w[l], conv_b[l],
                         dt_bias[l], a_log[l], d_skip[l], ssd_norm_g[l],
                         attn_norm_g[l], w_out[l])
        h = h + rmsnorm(m, mix_post_g[l])
        c = memory_cross_attention(rmsnorm(h, xa_pre_g[l]), rmsnorm(mem, mem_g[l]),
                                   xa_wq[l], xa_wkv[l], xa_wo[l])
        h = h + rmsnorm(c, xa_post_g[l])
        f = swiglu(rmsnorm(h, ffn2_pre_g[l]), ffn2_w_gu[l], ffn2_w_down[l])
        h = h + 0.5 * rmsnorm(f, ffn2_post_g[l])
    return h
```

```python
import functools

import jax
import jax.numpy as jnp
from jax import lax
from jax.experimental import pallas as pl
from jax.experimental.pallas import tpu as pltpu

F32 = jnp.float32
BF16 = jnp.bfloat16

EPS = 1e-6
LANES = 128
SUBLANES = 8
VMEM_LIMIT_BYTES = 56 * 1024 * 1024

SB_HEAD_DIM = 64
SB_TILE = 128
F32_EXP_UNDERFLOW = -104.0

SSD_HEAD_DIM = 64
SSD_GROUPS = 2
SSD_STATE = 128
SSD_CONV_K = 4
SSD_CHUNK = 128

XA_HEADS = 4


def _params(*sem):
    return pltpu.CompilerParams(dimension_semantics=sem, vmem_limit_bytes=VMEM_LIMIT_BYTES)


def _rms(x):
    return x * lax.rsqrt(jnp.mean(x * x, axis=-1, keepdims=True) + EPS)


def _dot(a, b):
    return jnp.dot(a, b, preferred_element_type=F32)


def _dot_nt(a, b):
    return lax.dot_general(a, b, (((1,), (1,)), ((), ())), preferred_element_type=F32)


def _silu(x):
    return x * jax.nn.sigmoid(x)


def _split_bf16(x, parts):
    out = []
    r = x
    for _ in range(parts - 1):
        p = r.astype(BF16)
        out.append(p)
        r = r - p.astype(F32)
    out.append(r.astype(BF16))
    return out


def _dot_split(x, w_bf16, parts, lhs=True):
    acc = None
    for p in _split_bf16(x, parts):
        t = _dot(p, w_bf16) if lhs else _dot(w_bf16, p)
        acc = t if acc is None else acc + t
    return acc


def _ffn_body(nf, h_ref, pre_ref, wg_ref, wu_ref, wd_ref, post_ref, o_ref, hn_ref, acc_ref):
    f = pl.program_id(1)

    @pl.when(f == 0)
    def _():
        hn_ref[...] = (_rms(h_ref[...]) * pre_ref[...]).astype(BF16)

    hn = hn_ref[...]
    g = _dot(hn, wg_ref[...])
    u = _dot(hn, wu_ref[...])
    part = _dot((_silu(g) * u).astype(BF16), wd_ref[...])

    @pl.when(f == 0)
    def _():
        acc_ref[...] = part

    @pl.when(f > 0)
    def _():
        acc_ref[...] += part

    @pl.when(f == nf - 1)
    def _():
        o_ref[...] = h_ref[...] + 0.5 * (_rms(acc_ref[...]) * post_ref[...])


def _ffn(h, pre_g, w_gu, w_down, post_g, *, tm=512, nf=2):
    t, d = h.shape
    d_ff = w_down.shape[0]
    tf = d_ff // nf
    assert tf * nf == d_ff and tf % LANES == 0 and t % tm == 0
    return pl.pallas_call(
        functools.partial(_ffn_body, nf),
        grid=(t // tm, nf),
        in_specs=[
            pl.BlockSpec((tm, d), lambda i, f: (i, 0)),
            pl.BlockSpec((1, d), lambda i, f: (0, 0)),
            pl.BlockSpec((d, tf), lambda i, f: (0, f)),
            pl.BlockSpec((d, tf), lambda i, f: (0, nf + f)),
            pl.BlockSpec((tf, d), lambda i, f: (f, 0)),
            pl.BlockSpec((1, d), lambda i, f: (0, 0)),
        ],
        out_specs=pl.BlockSpec((tm, d), lambda i, f: (i, 0)),
        out_shape=jax.ShapeDtypeStruct((t, d), F32),
        scratch_shapes=[pltpu.VMEM((tm, d), BF16), pltpu.VMEM((tm, d), F32)],
        compiler_params=_params("parallel", "arbitrary"),
        name="ffn",
    )(h, pre_g.reshape(1, d), w_gu, w_gu, w_down, post_g.reshape(1, d))


def _inproj_body(aw, sw, cw, h_ref, g_ref, w_ref, wdt_ref,
                 q_ref, k_ref, v_ref, z_ref, xbc_ref, dt_ref):
    hn = (_rms(h_ref[...]) * g_ref[...]).astype(BF16)
    scale = 1.0 / (SB_HEAD_DIM ** 0.5)
    q_ref[...] = (_dot(hn, w_ref[:, 0:aw]) * scale).astype(BF16)
    k_ref[...] = _dot(hn, w_ref[:, aw:2 * aw]).astype(BF16)
    v_ref[...] = _dot(hn, w_ref[:, 2 * aw:3 * aw]).astype(BF16)
    z_ref[...] = _dot(hn, w_ref[:, 3 * aw:3 * aw + sw]).astype(BF16)
    xbc_ref[...] = _dot(hn, w_ref[:, 3 * aw + sw:3 * aw + sw + cw])
    dt_ref[...] = _dot(hn, wdt_ref[...])


def _inproj(h, g, w_main, w_dt, aw, sw, cw, *, tm=512):
    t, d = h.shape
    n_main = w_main.shape[1]
    outs = [(aw, BF16), (aw, BF16), (aw, BF16), (sw, BF16), (cw, F32), (LANES, F32)]
    return pl.pallas_call(
        functools.partial(_inproj_body, aw, sw, cw),
        grid=(t // tm,),
        in_specs=[
            pl.BlockSpec((tm, d), lambda i: (i, 0)),
            pl.BlockSpec((1, d), lambda i: (0, 0)),
            pl.BlockSpec((d, n_main), lambda i: (0, 0)),
            pl.BlockSpec((d, LANES), lambda i: (0, 0)),
        ],
        out_specs=[pl.BlockSpec((tm, n), lambda i: (i, 0)) for n, _ in outs],
        out_shape=[jax.ShapeDtypeStruct((t, n), dt) for n, dt in outs],
        compiler_params=_params("parallel"),
        name="inproj",
    )(h, g.reshape(1, d), w_main, w_dt)


def _attn_body(q_ref, k_ref, v_ref, o_ref, acc_ref, c_ref):
    tt = SB_TILE
    qi = pl.program_id(2)
    q = q_ref[0]
    lane = lax.broadcasted_iota(jnp.int32, (tt, LANES), 1)
    zero = jnp.zeros_like(q)
    q_heads = (jnp.where(lane < SB_HEAD_DIM, q, zero), jnp.where(lane >= SB_HEAD_DIM, q, zero))
    row = lax.broadcasted_iota(jnp.int32, (tt, tt), 0)
    col = lax.broadcasted_iota(jnp.int32, (tt, tt), 1)
    causal = col < row
    r2 = lax.broadcasted_iota(jnp.int32, (tt, 2 * tt), 0)
    c2 = lax.broadcasted_iota(jnp.int32, (tt, 2 * tt), 1)
    u_aug = jnp.where((c2 >= tt) | (r2 > c2), 1.0, 0.0).astype(BF16)

    def tile(kb, masked):
        start = pl.multiple_of(kb * tt, tt)
        kblk = k_ref[0, pl.ds(start, tt), :]
        vblk = v_ref[0, pl.ds(start, tt), :]
        mx = None
        for hh in range(2):
            z = _dot_nt(q_heads[hh], kblk)
            nz = -z
            ls = jnp.minimum(nz, 0.0) - jnp.log(1.0 + jnp.exp(jnp.minimum(z, nz)))
            if masked:
                ls = jnp.where(causal, ls, 0.0)
            sa = _dot_split(ls, u_aug, 2)
            c = c_ref[hh]
            a = jnp.exp(z + ls + sa[:, :tt] + c)
            if masked:
                a = jnp.where(causal, a, 0.0)
            acc_ref[hh] += _dot(a.astype(BF16), vblk)
            c = c + sa[:, tt:]
            c_ref[hh] = c
            m = jnp.max(c)
            mx = m if mx is None else jnp.maximum(mx, m)
        return mx

    acc_ref[...] = jnp.zeros_like(acc_ref)
    c_ref[...] = jnp.zeros_like(c_ref)
    mx0 = tile(qi, True)

    def cond(s):
        kb, mx = s
        return jnp.logical_and(kb >= 0, mx > F32_EXP_UNDERFLOW)

    def body(s):
        kb, _ = s
        return kb - 1, tile(kb, False)

    lax.while_loop(cond, body, (qi - 1, mx0))
    o_ref[0] = jnp.where(lane < SB_HEAD_DIM, acc_ref[0], acc_ref[1]).astype(o_ref.dtype)


def _attention(q, k, v):
    b, s, w = q.shape
    tt = SB_TILE
    return pl.pallas_call(
        _attn_body,
        grid=(b, w // LANES, s // tt),
        in_specs=[
            pl.BlockSpec((1, tt, LANES), lambda bi, hp, qi: (bi, qi, hp)),
            pl.BlockSpec((1, s, LANES), lambda bi, hp, qi: (bi, 0, hp)),
            pl.BlockSpec((1, s, LANES), lambda bi, hp, qi: (bi, 0, hp)),
        ],
        out_specs=pl.BlockSpec((1, tt, LANES), lambda bi, hp, qi: (bi, qi, hp)),
        out_shape=jax.ShapeDtypeStruct((b, s, w), BF16),
        scratch_shapes=[pltpu.VMEM((2, tt, LANES), F32), pltpu.VMEM((2, tt, tt), F32)],
        compiler_params=_params("parallel", "parallel", "arbitrary"),
        name="sb_attn",
    )(q, k, v)


def _ssd_body(sw, xbc_ref, z_ref, dt_ref, cw_ref, cb_ref, dtb_ref, alog_ref, dskip_ref, ng_ref,
              o_ref, xpad_ref, st_ref):
    ll = SSD_CHUNK
    gn = SSD_GROUPS * SSD_STATE
    gw = sw // SSD_GROUPS
    pad = SUBLANES
    c = pl.program_id(1)

    @pl.when(c == 0)
    def _():
        xpad_ref[0:pad, :] = jnp.zeros((pad, xpad_ref.shape[1]), F32)
        st_ref[...] = jnp.zeros_like(st_ref)

    xpad_ref[pad:pad + ll, :] = xbc_ref[0]
    conv = cb_ref[...]
    for k in range(SSD_CONV_K):
        off = pad - (SSD_CONV_K - 1) + k
        conv = conv + cw_ref[k:k + 1, :] * xpad_ref[off:off + ll, :]
    xpad_ref[0:pad, :] = xpad_ref[ll:ll + pad, :]
    xbc = _silu(conv)
    xs = xbc[:, 0:sw]
    bm = xbc[:, sw:sw + gn]
    cm = xbc[:, sw + gn:sw + 2 * gn]

    dtr = dt_ref[0] + dtb_ref[...]
    dt = jnp.maximum(dtr, 0.0) + jnp.log1p(jnp.exp(-jnp.abs(dtr)))
    adt = dt * (-jnp.exp(alog_ref[...]))
    row = lax.broadcasted_iota(jnp.int32, (ll, ll), 0)
    col = lax.broadcasted_iota(jnp.int32, (ll, ll), 1)
    tri = col <= row
    a_cs = _dot_split(adt, jnp.where(tri, 1.0, 0.0).astype(BF16), 3, lhs=False)
    a_cs_t = a_cs.T
    a_last = a_cs[ll - 1:ll, :]
    er = lax.broadcasted_iota(jnp.int32, (LANES, sw), 0)
    ec = lax.broadcasted_iota(jnp.int32, (LANES, sw), 1)
    expand = jnp.where(ec // SSD_HEAD_DIM == er, 1.0, 0.0).astype(BF16)
    fac = jnp.concatenate([dt, jnp.exp(a_last - a_cs), jnp.exp(a_cs)], axis=0)
    fac_e = _dot_split(fac, expand, 2)
    dt_e, dec_e, ea_e = fac_e[0:ll], fac_e[ll:2 * ll], fac_e[2 * ll:3 * ll]

    xdt = xs * dt_e
    xdt_b = xdt.astype(BF16)
    xdec_b = (xdt * dec_e).astype(BF16)
    lane = lax.broadcasted_iota(jnp.int32, (ll, LANES), 1)
    heads_per_group = gw // SSD_HEAD_DIM
    ys = []
    for g in range(SSD_GROUPS):
        bg_t = bm[:, g * SSD_STATE:(g + 1) * SSD_STATE].T.astype(BF16)
        cg = cm[:, g * SSD_STATE:(g + 1) * SSD_STATE].astype(BF16)
        cb = _dot(cg, bg_t)
        pairs = []
        for p in range(heads_per_group // 2):
            res = []
            for e in range(2):
                hd = g * heads_per_group + 2 * p + e
                diff = a_cs[:, hd:hd + 1] - a_cs_t[hd:hd + 1, :]
                lm = jnp.exp(jnp.where(tri, diff, -jnp.inf))
                blk = (g * heads_per_group + 2 * p) * SSD_HEAD_DIM
                res.append(_dot((cb * lm).astype(BF16), xdt_b[:, blk:blk + LANES]))
            pairs.append(jnp.where(lane < SSD_HEAD_DIM, res[0], res[1]))
        y_diag = jnp.concatenate(pairs, axis=1)
        sl = slice(g * gw, (g + 1) * gw)
        st = st_ref[g]
        y_off = _dot(cg, st.astype(BF16)) * ea_e[:, sl]
        st_ref[g] = st * ea_e[ll - 1:ll, sl] + _dot(bg_t, xdec_b[:, sl])
        ys.append(y_diag + y_off)
    y = jnp.concatenate(ys, axis=1) + dskip_ref[...] * xs
    y = y * _silu(z_ref[0].astype(F32))
    outs = [_rms(y[:, g * gw:(g + 1) * gw]) for g in range(SSD_GROUPS)]
    o_ref[0] = (jnp.concatenate(outs, axis=1) * ng_ref[...]).astype(o_ref.dtype)


def _ssd(xbc, z, dt_raw, conv_w, conv_b, dt_bias, a_log, d_skip, norm_g):
    b, s, ch = xbc.shape
    sw = z.shape[2]
    nh = sw // SSD_HEAD_DIM
    ll = SSD_CHUNK

    def head_row(v):
        return jnp.zeros((1, LANES), F32).at[0, :nh].set(v)

    row_spec = lambda n: pl.BlockSpec((1, n), lambda bi, ci: (0, 0))
    return pl.pallas_call(
        functools.partial(_ssd_body, sw),
        grid=(b, s // ll),
        in_specs=[
            pl.BlockSpec((1, ll, ch), lambda bi, ci: (bi, ci, 0)),
            pl.BlockSpec((1, ll, sw), lambda bi, ci: (bi, ci, 0)),
            pl.BlockSpec((1, ll, LANES), lambda bi, ci: (bi, ci, 0)),
            pl.BlockSpec((SSD_CONV_K, ch), lambda bi, ci: (0, 0)),
            row_spec(ch), row_spec(LANES), row_spec(LANES), row_spec(sw), row_spec(sw),
        ],
        out_specs=pl.BlockSpec((1, ll, sw), lambda bi, ci: (bi, ci, 0)),
        out_shape=jax.ShapeDtypeStruct((b, s, sw), BF16),
        scratch_shapes=[pltpu.VMEM((ll + SUBLANES, ch), F32),
                        pltpu.VMEM((SSD_GROUPS, SSD_STATE, sw // SSD_GROUPS), F32)],
        compiler_params=_params("parallel", "arbitrary"),
        name="ssd",
    )(xbc, z, dt_raw, conv_w, conv_b.reshape(1, ch), head_row(dt_bias), head_row(a_log),
      jnp.repeat(d_skip, SSD_HEAD_DIM).reshape(1, sw), norm_g.reshape(1, sw))


def _outproj_body(oa_ref, os_ref, h_ref, ag_ref, wa_ref, ws_ref, pg_ref, o_ref):
    oan = (_rms(oa_ref[...].astype(F32)) * ag_ref[...]).astype(BF16)
    m = _dot(oan, wa_ref[...]) + _dot(os_ref[...], ws_ref[...])
    o_ref[...] = h_ref[...] + _rms(m) * pg_ref[...]


def _outproj(o_att, o_ssd, h, attn_g, w_out, post_g, *, tm=512):
    t, d = h.shape
    aw, sw = o_att.shape[1], o_ssd.shape[1]
    assert aw == sw
    return pl.pallas_call(
        _outproj_body,
        grid=(t // tm,),
        in_specs=[
            pl.BlockSpec((tm, aw), lambda i: (i, 0)),
            pl.BlockSpec((tm, sw), lambda i: (i, 0)),
            pl.BlockSpec((tm, d), lambda i: (i, 0)),
            pl.BlockSpec((1, aw), lambda i: (0, 0)),
            pl.BlockSpec((aw, d), lambda i: (0, 0)),
            pl.BlockSpec((sw, d), lambda i: (1, 0)),
            pl.BlockSpec((1, d), lambda i: (0, 0)),
        ],
        out_specs=pl.BlockSpec((tm, d), lambda i: (i, 0)),
        out_shape=jax.ShapeDtypeStruct((t, d), F32),
        compiler_params=_params("parallel"),
        name="outproj",
    )(o_att, o_ssd, h, attn_g.reshape(1, aw), w_out, w_out, post_g.reshape(1, d))


def _memkv_body(mem_ref, g_ref, w_ref, kv_ref):
    kv_ref[0] = _dot((_rms(mem_ref[0]) * g_ref[...]).astype(BF16), w_ref[...]).astype(BF16)


def _memkv(mem, g, wkv):
    b, m, d = mem.shape
    return pl.pallas_call(
        _memkv_body,
        grid=(b,),
        in_specs=[
            pl.BlockSpec((1, m, d), lambda bi: (bi, 0, 0)),
            pl.BlockSpec((1, d), lambda bi: (0, 0)),
            pl.BlockSpec((d, 2 * d), lambda bi: (0, 0)),
        ],
        out_specs=pl.BlockSpec((1, m, 2 * d), lambda bi: (bi, 0, 0)),
        out_shape=jax.ShapeDtypeStruct((b, m, 2 * d), BF16),
        compiler_params=_params("parallel"),
        name="memkv",
    )(mem, g.reshape(1, d), wkv)


def _xattn_body(h_ref, kv_ref, g_ref, wq_ref, wo_ref, pg_ref, o_ref):
    d = h_ref.shape[2]
    hd = d // XA_HEADS
    h = h_ref[0]
    hn = (_rms(h) * g_ref[...]).astype(BF16)
    q = (_dot(hn, wq_ref[...]) * (1.0 / hd ** 0.5)).astype(BF16)
    outs = []
    for i in range(XA_HEADS):
        sc = _dot_nt(q[:, i * hd:(i + 1) * hd], kv_ref[0, :, i * hd:(i + 1) * hd])
        e = jnp.exp(sc - jnp.max(sc, axis=-1, keepdims=True))
        p = e / jnp.sum(e, axis=-1, keepdims=True)
        outs.append(_dot(p.astype(BF16), kv_ref[0, :, d + i * hd:d + (i + 1) * hd]).astype(BF16))
    c = _dot(jnp.concatenate(outs, axis=1), wo_ref[...])
    o_ref[0] = h + _rms(c) * pg_ref[...]


def _xattn(h, kv, pre_g, wq, wo, post_g, *, tm=512):
    b, s, d = h.shape
    m = kv.shape[1]
    return pl.pallas_call(
        _xattn_body,
        grid=(b, s // tm),
        in_specs=[
            pl.BlockSpec((1, tm, d), lambda bi, i: (bi, i, 0)),
            pl.BlockSpec((1, m, 2 * d), lambda bi, i: (bi, 0, 0)),
            pl.BlockSpec((1, d), lambda bi, i: (0, 0)),
            pl.BlockSpec((d, d), lambda bi, i: (0, 0)),
            pl.BlockSpec((d, d), lambda bi, i: (0, 0)),
            pl.BlockSpec((1, d), lambda bi, i: (0, 0)),
        ],
        out_specs=pl.BlockSpec((1, tm, d), lambda bi, i: (bi, i, 0)),
        out_shape=jax.ShapeDtypeStruct((b, s, d), F32),
        compiler_params=_params("parallel", "parallel"),
        name="xattn",
    )(h, kv, pre_g.reshape(1, d), wq, wo, post_g.reshape(1, d))


def _mixer(h, b, s, mix_pre_g, w_in, conv_w, conv_b, dt_bias, a_log, d_skip, ssd_norm_g,
           attn_norm_g, w_out, mix_post_g):
    d = h.shape[1]
    nh = dt_bias.shape[0]
    sw = nh * SSD_HEAD_DIM
    cw = conv_w.shape[1]
    aw = (w_in.shape[1] - nh - sw - cw) // 3
    n_main = 3 * aw + sw + cw
    w_main = w_in[:, :n_main].astype(BF16)
    w_dt = jnp.zeros((d, LANES), BF16).at[:, :nh].set(w_in[:, n_main:].astype(BF16))
    q, k, v, z, xbc, dt_raw = _inproj(h, mix_pre_g, w_main, w_dt, aw, sw, cw)
    o_att = _attention(q.reshape(b, s, aw), k.reshape(b, s, aw), v.reshape(b, s, aw))
    o_ssd = _ssd(xbc.reshape(b, s, cw), z.reshape(b, s, sw), dt_raw.reshape(b, s, LANES),
                 conv_w, conv_b, dt_bias, a_log, d_skip, ssd_norm_g)
    return _outproj(o_att.reshape(b * s, aw), o_ssd.reshape(b * s, sw), h, attn_norm_g,
                    w_out.astype(BF16), mix_post_g)


def kernel(x, mem, ffn1_pre_g, ffn1_w_gu, ffn1_w_down, ffn1_post_g, mix_pre_g, w_in, conv_w, conv_b, dt_bias, a_log, d_skip, ssd_norm_g, attn_norm_g, w_out, mix_post_g, xa_pre_g, mem_g, xa_wq, xa_wkv, xa_wo, xa_post_g, ffn2_pre_g, ffn2_w_gu, ffn2_w_down, ffn2_post_g):
    b, s, d = x.shape
    h = x.reshape(b * s, d)
    for l in range(w_in.shape[0]):
        h = _ffn(h, ffn1_pre_g[l], ffn1_w_gu[l].astype(BF16), ffn1_w_down[l].astype(BF16),
                 ffn1_post_g[l])
        h = _mixer(h, b, s, mix_pre_g[l], w_in[l], conv_w[l], conv_b[l], dt_bias[l], a_log[l],
                   d_skip[l], ssd_norm_g[l], attn_norm_g[l], w_out[l], mix_post_g[l])
        kv = _memkv(mem, mem_g[l], xa_wkv[l].astype(BF16))
        h = _xattn(h.reshape(b, s, d), kv, xa_pre_g[l], xa_wq[l].astype(BF16),
                   xa_wo[l].astype(BF16), xa_post_g[l]).reshape(b * s, d)
        h = _ffn(h, ffn2_pre_g[l], ffn2_w_gu[l].astype(BF16), ffn2_w_down[l].astype(BF16),
                 ffn2_post_g[l])
    return h.reshape(b, s, d)
```

```python
import functools

import jax
import jax.numpy as jnp
from jax import lax
from jax.experimental import pallas as pl
from jax.experimental.pallas import tpu as pltpu

F32 = jnp.float32
BF16 = jnp.bfloat16

EPS = 1e-6
LANES = 128
SUBLANES = 8
VMEM_LIMIT_BYTES = 56 * 1024 * 1024

SB_HEAD_DIM = 64
SB_TILE = 256
LOG2_E = 1.4426950408889634
F32_EXP2_UNDERFLOW = -150.0

SSD_HEAD_DIM = 64
SSD_GROUPS = 2
SSD_STATE = 128
SSD_CONV_K = 4
SSD_CHUNK = 128

XA_HEADS = 4


def _params(*sem):
    return pltpu.CompilerParams(dimension_semantics=sem, vmem_limit_bytes=VMEM_LIMIT_BYTES)


def _rms(x):
    return x * lax.rsqrt(jnp.mean(x * x, axis=-1, keepdims=True) + EPS)


def _dot(a, b):
    return jnp.dot(a, b, preferred_element_type=F32)


def _dot_nt(a, b):
    return lax.dot_general(a, b, (((1,), (1,)), ((), ())), preferred_element_type=F32)


def _silu(x):
    return x * jax.nn.sigmoid(x)


def _split_bf16(x, parts):
    out = []
    r = x
    for _ in range(parts - 1):
        p = r.astype(BF16)
        out.append(p)
        r = r - p.astype(F32)
    out.append(r.astype(BF16))
    return out


def _dot_split(x, w_bf16, parts, lhs=True):
    acc = None
    for p in _split_bf16(x, parts):
        t = _dot(p, w_bf16) if lhs else _dot(w_bf16, p)
        acc = t if acc is None else acc + t
    return acc


def _ffn_body(nf, h_ref, pre_ref, wg_ref, wu_ref, wd_ref, post_ref, o_ref, hn_ref, acc_ref):
    f = pl.program_id(1)

    @pl.when(f == 0)
    def _():
        hn_ref[...] = (_rms(h_ref[...]) * pre_ref[...]).astype(BF16)

    hn = hn_ref[...]
    g = _dot(hn, wg_ref[...])
    u = _dot(hn, wu_ref[...])
    part = _dot((_silu(g) * u).astype(BF16), wd_ref[...])

    @pl.when(f == 0)
    def _():
        acc_ref[...] = part

    @pl.when(f > 0)
    def _():
        acc_ref[...] += part

    @pl.when(f == nf - 1)
    def _():
        o_ref[...] = h_ref[...] + 0.5 * (_rms(acc_ref[...]) * post_ref[...])


def _ffn(h, pre_g, w_gu, w_down, post_g, *, tm=512, nf=2):
    t, d = h.shape
    d_ff = w_down.shape[0]
    tf = d_ff // nf
    assert tf * nf == d_ff and tf % LANES == 0 and t % tm == 0
    return pl.pallas_call(
        functools.partial(_ffn_body, nf),
        grid=(t // tm, nf),
        in_specs=[
            pl.BlockSpec((tm, d), lambda i, f: (i, 0)),
            pl.BlockSpec((1, d), lambda i, f: (0, 0)),
            pl.BlockSpec((d, tf), lambda i, f: (0, f)),
            pl.BlockSpec((d, tf), lambda i, f: (0, nf + f)),
            pl.BlockSpec((tf, d), lambda i, f: (f, 0)),
            pl.BlockSpec((1, d), lambda i, f: (0, 0)),
        ],
        out_specs=pl.BlockSpec((tm, d), lambda i, f: (i, 0)),
        out_shape=jax.ShapeDtypeStruct((t, d), F32),
        scratch_shapes=[pltpu.VMEM((tm, d), BF16), pltpu.VMEM((tm, d), F32)],
        compiler_params=_params("parallel", "arbitrary"),
        name="ffn",
    )(h, pre_g.reshape(1, d), w_gu, w_gu, w_down, post_g.reshape(1, d))


def _inproj_body(aw, sw, cw, h_ref, g_ref, w_ref, wdt_ref,
                 q_ref, k_ref, v_ref, z_ref, xbc_ref, dt_ref):
    hn = (_rms(h_ref[...]) * g_ref[...]).astype(BF16)
    scale = LOG2_E / (SB_HEAD_DIM ** 0.5)
    q_ref[...] = (_dot(hn, w_ref[:, 0:aw]) * scale).astype(BF16)
    k_ref[...] = _dot(hn, w_ref[:, aw:2 * aw]).astype(BF16)
    v_ref[...] = _dot(hn, w_ref[:, 2 * aw:3 * aw]).astype(BF16)
    z_ref[...] = _dot(hn, w_ref[:, 3 * aw:3 * aw + sw]).astype(BF16)
    xbc_ref[...] = _dot(hn, w_ref[:, 3 * aw + sw:3 * aw + sw + cw])
    dt_ref[...] = _dot(hn, wdt_ref[...])


def _inproj(h, g, w_main, w_dt, aw, sw, cw, *, tm=512):
    t, d = h.shape
    n_main = w_main.shape[1]
    outs = [(aw, BF16), (aw, BF16), (aw, BF16), (sw, BF16), (cw, F32), (LANES, F32)]
    return pl.pallas_call(
        functools.partial(_inproj_body, aw, sw, cw),
        grid=(t // tm,),
        in_specs=[
            pl.BlockSpec((tm, d), lambda i: (i, 0)),
            pl.BlockSpec((1, d), lambda i: (0, 0)),
            pl.BlockSpec((d, n_main), lambda i: (0, 0)),
            pl.BlockSpec((d, LANES), lambda i: (0, 0)),
        ],
        out_specs=[pl.BlockSpec((tm, n), lambda i: (i, 0)) for n, _ in outs],
        out_shape=[jax.ShapeDtypeStruct((t, n), dt) for n, dt in outs],
        compiler_params=_params("parallel"),
        name="inproj",
    )(h, g.reshape(1, d), w_main, w_dt)


def _attn_body(q_ref, k_ref, v_ref, o_ref, acc_ref, c_ref):
    tt = SB_TILE
    qi = pl.program_id(2)
    q = q_ref[0]
    lane = lax.broadcasted_iota(jnp.int32, (tt, LANES), 1)
    zero = jnp.zeros_like(q)
    q_heads = (jnp.where(lane < SB_HEAD_DIM, q, zero), jnp.where(lane >= SB_HEAD_DIM, q, zero))
    row = lax.broadcasted_iota(jnp.int32, (tt, tt), 0)
    col = lax.broadcasted_iota(jnp.int32, (tt, tt), 1)
    causal = col < row
    upper = jnp.where(row > col, 1.0, 0.0).astype(BF16)

    def load_kv(kb):
        start = pl.multiple_of(kb * tt, tt)
        return k_ref[0, pl.ds(start, tt), :], v_ref[0, pl.ds(start, tt), :]

    def block(hh, kblk, vblk, c_in, masked):
        z = _dot_nt(q_heads[hh], kblk)
        nz = -z
        ls = jnp.minimum(nz, 0.0) - jnp.log2(1.0 + jnp.exp2(jnp.minimum(z, nz)))
        if masked:
            ls = jnp.where(causal, ls, 0.0)
        e = z + ls + _dot_split(ls, upper, 2)
        if c_in is not None:
            e = e + c_in
        a = jnp.exp2(e)
        if masked:
            a = jnp.where(causal, a, 0.0)
        return _dot(a.astype(BF16), vblk), jnp.sum(ls, axis=1, keepdims=True)

    kd, vd = load_kv(qi)
    kp, vp = load_kv(jnp.maximum(qi - 1, 0))
    penalty = jnp.where(qi > 0, 0.0, -1e30).astype(F32)
    mx = None
    for hh in range(2):
        o_d, tot_d = block(hh, kd, vd, None, True)
        c1 = tot_d + penalty
        o_p, tot_p = block(hh, kp, vp, c1, False)
        acc_ref[hh] = o_d + o_p
        c2 = c1 + tot_p
        c_ref[hh] = c2
        m = jnp.max(c2)
        mx = m if mx is None else jnp.maximum(mx, m)

    def cond(s):
        kb, mx = s
        return jnp.logical_and(kb >= 0, mx > F32_EXP2_UNDERFLOW)

    def body(s):
        kb, _ = s
        kblk, vblk = load_kv(kb)
        mx = None
        for hh in range(2):
            c = c_ref[hh]
            o, tot = block(hh, kblk, vblk, c, False)
            acc_ref[hh] += o
            c = c + tot
            c_ref[hh] = c
            m = jnp.max(c)
            mx = m if mx is None else jnp.maximum(mx, m)
        return kb - 1, mx

    lax.while_loop(cond, body, (qi - 2, mx))
    o_ref[0] = jnp.where(lane < SB_HEAD_DIM, acc_ref[0], acc_ref[1]).astype(o_ref.dtype)


def _attention(q, k, v):
    b, s, w = q.shape
    tt = SB_TILE
    return pl.pallas_call(
        _attn_body,
        grid=(b, w // LANES, s // tt),
        in_specs=[
            pl.BlockSpec((1, tt, LANES), lambda bi, hp, qi: (bi, qi, hp)),
            pl.BlockSpec((1, s, LANES), lambda bi, hp, qi: (bi, 0, hp)),
            pl.BlockSpec((1, s, LANES), lambda bi, hp, qi: (bi, 0, hp)),
        ],
        out_specs=pl.BlockSpec((1, tt, LANES), lambda bi, hp, qi: (bi, qi, hp)),
        out_shape=jax.ShapeDtypeStruct((b, s, w), BF16),
        scratch_shapes=[pltpu.VMEM((2, tt, LANES), F32), pltpu.VMEM((2, tt, 1), F32)],
        compiler_params=_params("parallel", "parallel", "arbitrary"),
        name="sb_attn",
    )(q, k, v)


def _ssd_body(sw, xbc_ref, z_ref, dt_ref, cw_ref, cb_ref, dtb_ref, alog_ref, dskip_ref, ng_ref,
              o_ref, xpad_ref, st_ref):
    ll = SSD_CHUNK
    gn = SSD_GROUPS * SSD_STATE
    gw = sw // SSD_GROUPS
    pad = SUBLANES
    c = pl.program_id(1)

    @pl.when(c == 0)
    def _():
        xpad_ref[0:pad, :] = jnp.zeros((pad, xpad_ref.shape[1]), F32)
        st_ref[...] = jnp.zeros_like(st_ref)

    xpad_ref[pad:pad + ll, :] = xbc_ref[0]
    conv = cb_ref[...]
    for k in range(SSD_CONV_K):
        off = pad - (SSD_CONV_K - 1) + k
        conv = conv + cw_ref[k:k + 1, :] * xpad_ref[off:off + ll, :]
    xpad_ref[0:pad, :] = xpad_ref[ll:ll + pad, :]
    xbc = _silu(conv)
    xs = xbc[:, 0:sw]
    bm = xbc[:, sw:sw + gn]
    cm = xbc[:, sw + gn:sw + 2 * gn]

    dtr = dt_ref[0] + dtb_ref[...]
    dt = jnp.maximum(dtr, 0.0) + jnp.log1p(jnp.exp(-jnp.abs(dtr)))
    adt = dt * (-jnp.exp(alog_ref[...]))
    row = lax.broadcasted_iota(jnp.int32, (ll, ll), 0)
    col = lax.broadcasted_iota(jnp.int32, (ll, ll), 1)
    tri = col <= row
    a_cs = _dot_split(adt, jnp.where(tri, 1.0, 0.0).astype(BF16), 3, lhs=False)
    a_cs_t = a_cs.T
    a_last = a_cs[ll - 1:ll, :]
    er = lax.broadcasted_iota(jnp.int32, (LANES, sw), 0)
    ec = lax.broadcasted_iota(jnp.int32, (LANES, sw), 1)
    expand = jnp.where(ec // SSD_HEAD_DIM == er, 1.0, 0.0).astype(BF16)
    fac = jnp.concatenate([dt, jnp.exp(a_last - a_cs), jnp.exp(a_cs)], axis=0)
    fac_e = _dot_split(fac, expand, 2)
    dt_e, dec_e, ea_e = fac_e[0:ll], fac_e[ll:2 * ll], fac_e[2 * ll:3 * ll]

    xdt = xs * dt_e
    xdt_b = xdt.astype(BF16)
    xdec_b = (xdt * dec_e).astype(BF16)
    lane = lax.broadcasted_iota(jnp.int32, (ll, LANES), 1)
    heads_per_group = gw // SSD_HEAD_DIM
    ys = []
    for g in range(SSD_GROUPS):
        bg_t = bm[:, g * SSD_STATE:(g + 1) * SSD_STATE].T.astype(BF16)
        cg = cm[:, g * SSD_STATE:(g + 1) * SSD_STATE].astype(BF16)
        cb = _dot(cg, bg_t)
        pairs = []
        for p in range(heads_per_group // 2):
            res = []
            for e in range(2):
                hd = g * heads_per_group + 2 * p + e
                diff = a_cs[:, hd:hd + 1] - a_cs_t[hd:hd + 1, :]
                lm = jnp.exp(jnp.where(tri, diff, -jnp.inf))
                blk = (g * heads_per_group + 2 * p) * SSD_HEAD_DIM
                res.append(_dot((cb * lm).astype(BF16), xdt_b[:, blk:blk + LANES]))
            pairs.append(jnp.where(lane < SSD_HEAD_DIM, res[0], res[1]))
        y_diag = jnp.concatenate(pairs, axis=1)
        sl = slice(g * gw, (g + 1) * gw)
        st = st_ref[g]
        y_off = _dot(cg, st.astype(BF16)) * ea_e[:, sl]
        st_ref[g] = st * ea_e[ll - 1:ll, sl] + _dot(bg_t, xdec_b[:, sl])
        ys.append(y_diag + y_off)
    y = jnp.concatenate(ys, axis=1) + dskip_ref[...] * xs
    y = y * _silu(z_ref[0].astype(F32))
    outs = [_rms(y[:, g * gw:(g + 1) * gw]) for g in range(SSD_GROUPS)]
    o_ref[0] = (jnp.concatenate(outs, axis=1) * ng_ref[...]).astype(o_ref.dtype)


def _ssd(xbc, z, dt_raw, conv_w, conv_b, dt_bias, a_log, d_skip, norm_g):
    b, s, ch = xbc.shape
    sw = z.shape[2]
    nh = sw // SSD_HEAD_DIM
    ll = SSD_CHUNK

    def head_row(v):
        return jnp.zeros((1, LANES), F32).at[0, :nh].set(v)

    row_spec = lambda n: pl.BlockSpec((1, n), lambda bi, ci: (0, 0))
    return pl.pallas_call(
        functools.partial(_ssd_body, sw),
        grid=(b, s // ll),
        in_specs=[
            pl.BlockSpec((1, ll, ch), lambda bi, ci: (bi, ci, 0)),
            pl.BlockSpec((1, ll, sw), lambda bi, ci: (bi, ci, 0)),
            pl.BlockSpec((1, ll, LANES), lambda bi, ci: (bi, ci, 0)),
            pl.BlockSpec((SSD_CONV_K, ch), lambda bi, ci: (0, 0)),
            row_spec(ch), row_spec(LANES), row_spec(LANES), row_spec(sw), row_spec(sw),
        ],
        out_specs=pl.BlockSpec((1, ll, sw), lambda bi, ci: (bi, ci, 0)),
        out_shape=jax.ShapeDtypeStruct((b, s, sw), BF16),
        scratch_shapes=[pltpu.VMEM((ll + SUBLANES, ch), F32),
                        pltpu.VMEM((SSD_GROUPS, SSD_STATE, sw // SSD_GROUPS), F32)],
        compiler_params=_params("parallel", "arbitrary"),
        name="ssd",
    )(xbc, z, dt_raw, conv_w, conv_b.reshape(1, ch), head_row(dt_bias), head_row(a_log),
      jnp.repeat(d_skip, SSD_HEAD_DIM).reshape(1, sw), norm_g.reshape(1, sw))


def _outproj_body(oa_ref, os_ref, h_ref, ag_ref, wa_ref, ws_ref, pg_ref, o_ref):
    oan = (_rms(oa_ref[...].astype(F32)) * ag_ref[...]).astype(BF16)
    m = _dot(oan, wa_ref[...]) + _dot(os_ref[...], ws_ref[...])
    o_ref[...] = h_ref[...] + _rms(m) * pg_ref[...]


def _outproj(o_att, o_ssd, h, attn_g, w_out, post_g, *, tm=512):
    t, d = h.shape
    aw, sw = o_att.shape[1], o_ssd.shape[1]
    assert aw == sw
    return pl.pallas_call(
        _outproj_body,
        grid=(t // tm,),
        in_specs=[
            pl.BlockSpec((tm, aw), lambda i: (i, 0)),
            pl.BlockSpec((tm, sw), lambda i: (i, 0)),
            pl.BlockSpec((tm, d), lambda i: (i, 0)),
            pl.BlockSpec((1, aw), lambda i: (0, 0)),
            pl.BlockSpec((aw, d), lambda i: (0, 0)),
            pl.BlockSpec((sw, d), lambda i: (1, 0)),
            pl.BlockSpec((1, d), lambda i: (0, 0)),
        ],
        out_specs=pl.BlockSpec((tm, d), lambda i: (i, 0)),
        out_shape=jax.ShapeDtypeStruct((t, d), F32),
        compiler_params=_params("parallel"),
        name="outproj",
    )(o_att, o_ssd, h, attn_g.reshape(1, aw), w_out, w_out, post_g.reshape(1, d))


def _memkv_body(mem_ref, g_ref, w_ref, kv_ref):
    kv_ref[0] = _dot((_rms(mem_ref[0]) * g_ref[...]).astype(BF16), w_ref[...]).astype(BF16)


def _memkv(mem, g, wkv):
    b, m, d = mem.shape
    return pl.pallas_call(
        _memkv_body,
        grid=(b,),
        in_specs=[
            pl.BlockSpec((1, m, d), lambda bi: (bi, 0, 0)),
            pl.BlockSpec((1, d), lambda bi: (0, 0)),
            pl.BlockSpec((d, 2 * d), lambda bi: (0, 0)),
        ],
        out_specs=pl.BlockSpec((1, m, 2 * d), lambda bi: (bi, 0, 0)),
        out_shape=jax.ShapeDtypeStruct((b, m, 2 * d), BF16),
        compiler_params=_params("parallel"),
        name="memkv",
    )(mem, g.reshape(1, d), wkv)


def _xattn_body(h_ref, kv_ref, g_ref, wq_ref, wo_ref, pg_ref, o_ref):
    d = h_ref.shape[2]
    hd = d // XA_HEADS
    h = h_ref[0]
    hn = (_rms(h) * g_ref[...]).astype(BF16)
    q = (_dot(hn, wq_ref[...]) * (1.0 / hd ** 0.5)).astype(BF16)
    outs = []
    for i in range(XA_HEADS):
        sc = _dot_nt(q[:, i * hd:(i + 1) * hd], kv_ref[0, :, i * hd:(i + 1) * hd])
        e = jnp.exp(sc - jnp.max(sc, axis=-1, keepdims=True))
        p = e / jnp.sum(e, axis=-1, keepdims=True)
        outs.append(_dot(p.astype(BF16), kv_ref[0, :, d + i * hd:d + (i + 1) * hd]).astype(BF16))
    c = _dot(jnp.concatenate(outs, axis=1), wo_ref[...])
    o_ref[0] = h + _rms(c) * pg_ref[...]


def _xattn(h, kv, pre_g, wq, wo, post_g, *, tm=512):
    b, s, d = h.shape
    m = kv.shape[1]
    return pl.pallas_call(
        _xattn_body,
        grid=(b, s // tm),
        in_specs=[
            pl.BlockSpec((1, tm, d), lambda bi, i: (bi, i, 0)),
            pl.BlockSpec((1, m, 2 * d), lambda bi, i: (bi, 0, 0)),
            pl.BlockSpec((1, d), lambda bi, i: (0, 0)),
            pl.BlockSpec((d, d), lambda bi, i: (0, 0)),
            pl.BlockSpec((d, d), lambda bi, i: (0, 0)),
            pl.BlockSpec((1, d), lambda bi, i: (0, 0)),
        ],
        out_specs=pl.BlockSpec((1, tm, d), lambda bi, i: (bi, i, 0)),
        out_shape=jax.ShapeDtypeStruct((b, s, d), F32),
        compiler_params=_params("parallel", "parallel"),
        name="xattn",
    )(h, kv, pre_g.reshape(1, d), wq, wo, post_g.reshape(1, d))


def _mixer(h, b, s, mix_pre_g, w_in, conv_w, conv_b, dt_bias, a_log, d_skip, ssd_norm_g,
           attn_norm_g, w_out, mix_post_g):
    d = h.shape[1]
    nh = dt_bias.shape[0]
    sw = nh * SSD_HEAD_DIM
    cw = conv_w.shape[1]
    aw = (w_in.shape[1] - nh - sw - cw) // 3
    n_main = 3 * aw + sw + cw
    w_main = w_in[:, :n_main].astype(BF16)
    w_dt = jnp.zeros((d, LANES), BF16).at[:, :nh].set(w_in[:, n_main:].astype(BF16))
    q, k, v, z, xbc, dt_raw = _inproj(h, mix_pre_g, w_main, w_dt, aw, sw, cw)
    o_att = _attention(q.reshape(b, s, aw), k.reshape(b, s, aw), v.reshape(b, s, aw))
    o_ssd = _ssd(xbc.reshape(b, s, cw), z.reshape(b, s, sw), dt_raw.reshape(b, s, LANES),
                 conv_w, conv_b, dt_bias, a_log, d_skip, ssd_norm_g)
    return _outproj(o_att.reshape(b * s, aw), o_ssd.reshape(b * s, sw), h, attn_norm_g,
                    w_out.astype(BF16), mix_post_g)


def kernel(x, mem, ffn1_pre_g, ffn1_w_gu, ffn1_w_down, ffn1_post_g, mix_pre_g, w_in, conv_w, conv_b, dt_bias, a_log, d_skip, ssd_norm_g, attn_norm_g, w_out, mix_post_g, xa_pre_g, mem_g, xa_wq, xa_wkv, xa_wo, xa_post_g, ffn2_pre_g, ffn2_w_gu, ffn2_w_down, ffn2_post_g):
    b, s, d = x.shape
    h = x.reshape(b * s, d)
    for l in range(w_in.shape[0]):
        h = _ffn(h, ffn1_pre_g[l], ffn1_w_gu[l].astype(BF16), ffn1_w_down[l].astype(BF16),
                 ffn1_post_g[l])
        h = _mixer(h, b, s, mix_pre_g[l], w_in[l], conv_w[l], conv_b[l], dt_bias[l], a_log[l],
                   d_skip[l], ssd_norm_g[l], attn_norm_g[l], w_out[l], mix_post_g[l])
        kv = _memkv(mem, mem_g[l], xa_wkv[l].astype(BF16))
        h = _xattn(h.reshape(b, s, d), kv, xa_pre_g[l], xa_wq[l].astype(BF16),
                   xa_wo[l].astype(BF16), xa_post_g[l]).reshape(b * s, d)
        h = _ffn(h, ffn2_pre_g[l], ffn2_w_gu[l].astype(BF16), ffn2_w_down[l].astype(BF16),
                 ffn2_post_g[l])
    return h.reshape(b, s, d)
```

```python
import functools

import jax
import jax.numpy as jnp
from jax import lax
from jax.experimental import pallas as pl
from jax.experimental.pallas import tpu as pltpu

F32 = jnp.float32
BF16 = jnp.bfloat16

EPS = 1e-6
LANES = 128
SUBLANES = 8
VMEM_LIMIT_BYTES = 56 * 1024 * 1024

SB_HEAD_DIM = 64
SB_TILE = 256
SB_PAIRS = 4
LOG2_E = 1.4426950408889634
EXP2_UNDERFLOW = 150.0
EXP2_CLAMP = 126.0

SSD_HEAD_DIM = 64
SSD_GROUPS = 2
SSD_STATE = 128
SSD_CONV_K = 4
SSD_CHUNK = 128

XA_HEADS = 4


def _params(*sem):
    return pltpu.CompilerParams(dimension_semantics=sem, vmem_limit_bytes=VMEM_LIMIT_BYTES)


def _rms(x):
    return x * lax.rsqrt(jnp.mean(x * x, axis=-1, keepdims=True) + EPS)


def _dot(a, b):
    return jnp.dot(a, b, preferred_element_type=F32)


def _dot_nt(a, b):
    return lax.dot_general(a, b, (((1,), (1,)), ((), ())), preferred_element_type=F32)


def _silu(x):
    return x * jax.nn.sigmoid(x)


def _split_bf16(x, parts):
    out = []
    r = x
    for _ in range(parts - 1):
        p = r.astype(BF16)
        out.append(p)
        r = r - p.astype(F32)
    out.append(r.astype(BF16))
    return out


def _dot_split(x, w_bf16, parts, lhs=True):
    acc = None
    for p in _split_bf16(x, parts):
        t = _dot(p, w_bf16) if lhs else _dot(w_bf16, p)
        acc = t if acc is None else acc + t
    return acc


def _ffn_body(h_ref, pre_ref, wgu_ref, wd_ref, post_ref, o_ref):
    d_ff = wd_ref.shape[0]
    h = h_ref[...]
    hn = (_rms(h) * pre_ref[...]).astype(BF16)
    g = _dot(hn, wgu_ref[:, :d_ff])
    u = _dot(hn, wgu_ref[:, d_ff:])
    f = _dot((_silu(g) * u).astype(BF16), wd_ref[...])
    o_ref[...] = h + 0.5 * (_rms(f) * post_ref[...])


def _ffn(h, pre_g, w_gu, w_down, post_g, *, tm=512):
    t, d = h.shape
    d_ff = w_down.shape[0]
    return pl.pallas_call(
        _ffn_body,
        grid=(t // tm,),
        in_specs=[
            pl.BlockSpec((tm, d), lambda i: (i, 0)),
            pl.BlockSpec((1, d), lambda i: (0, 0)),
            pl.BlockSpec((d, 2 * d_ff), lambda i: (0, 0)),
            pl.BlockSpec((d_ff, d), lambda i: (0, 0)),
            pl.BlockSpec((1, d), lambda i: (0, 0)),
        ],
        out_specs=pl.BlockSpec((tm, d), lambda i: (i, 0)),
        out_shape=jax.ShapeDtypeStruct((t, d), F32),
        compiler_params=_params("parallel"),
        name="ffn",
    )(h, pre_g.reshape(1, d), w_gu, w_down, post_g.reshape(1, d))


def _inproj_body(aw, sw, cw, h_ref, g_ref, w_ref, wdt_ref,
                 q_ref, k_ref, v_ref, z_ref, xbc_ref, dt_ref):
    hn = (_rms(h_ref[...]) * g_ref[...]).astype(BF16)
    scale = LOG2_E / (SB_HEAD_DIM ** 0.5)
    q_ref[...] = (_dot(hn, w_ref[:, 0:aw]) * scale).astype(BF16)
    k_ref[...] = _dot(hn, w_ref[:, aw:2 * aw]).astype(BF16)
    v_ref[...] = _dot(hn, w_ref[:, 2 * aw:3 * aw]).astype(BF16)
    z_ref[...] = _dot(hn, w_ref[:, 3 * aw:3 * aw + sw]).astype(BF16)
    xbc_ref[...] = _dot(hn, w_ref[:, 3 * aw + sw:3 * aw + sw + cw])
    dt_ref[...] = _dot(hn, wdt_ref[...])


def _inproj(h, g, w_main, w_dt, aw, sw, cw, *, tm=512):
    t, d = h.shape
    n_main = w_main.shape[1]
    outs = [(aw, BF16), (aw, BF16), (aw, BF16), (sw, BF16), (cw, F32), (LANES, F32)]
    return pl.pallas_call(
        functools.partial(_inproj_body, aw, sw, cw),
        grid=(t // tm,),
        in_specs=[
            pl.BlockSpec((tm, d), lambda i: (i, 0)),
            pl.BlockSpec((1, d), lambda i: (0, 0)),
            pl.BlockSpec((d, n_main), lambda i: (0, 0)),
            pl.BlockSpec((d, LANES), lambda i: (0, 0)),
        ],
        out_specs=[pl.BlockSpec((tm, n), lambda i: (i, 0)) for n, _ in outs],
        out_shape=[jax.ShapeDtypeStruct((t, n), dt) for n, dt in outs],
        compiler_params=_params("parallel"),
        name="inproj",
    )(h, g.reshape(1, d), w_main, w_dt)


def _attn_body(q_ref, k_ref, v_ref, o_ref, acc_ref, c_ref):
    tt = SB_TILE
    qi = pl.program_id(2)
    lane = lax.broadcasted_iota(jnp.int32, (tt, LANES), 1)
    first = lane < SB_HEAD_DIM
    q_pairs = []
    for pr in range(SB_PAIRS):
        q = q_ref[0, :, pr * LANES:(pr + 1) * LANES]
        zero = jnp.zeros_like(q)
        q_pairs.append(jnp.concatenate([jnp.where(first, q, zero), jnp.where(first, zero, q)], axis=0))
    row = lax.broadcasted_iota(jnp.int32, (2 * tt, tt), 0)
    col = lax.broadcasted_iota(jnp.int32, (2 * tt, tt), 1)
    causal = col < jnp.where(row >= tt, row - tt, row)
    urow = lax.broadcasted_iota(jnp.int32, (tt, tt), 0)
    ucol = lax.broadcasted_iota(jnp.int32, (tt, tt), 1)
    upper = jnp.where(urow > ucol, 1.0, 0.0).astype(BF16)

    def block(pr, kb, c_in, masked):
        start = pl.multiple_of(kb * tt, tt)
        sl = pl.ds(pr * LANES, LANES)
        kblk = k_ref[0, pl.ds(start, tt), sl]
        vblk = v_ref[0, pl.ds(start, tt), sl]
        zero = jnp.zeros_like(vblk)
        v_bd = jnp.concatenate([jnp.where(first, vblk, zero), jnp.where(first, zero, vblk)], axis=0)
        z = jnp.minimum(_dot_nt(q_pairs[pr], kblk), EXP2_CLAMP)
        lp = jnp.log2(1.0 + jnp.exp2(z))
        if masked:
            lp = jnp.where(causal, lp, 0.0)
        e = z - lp - _dot(lp.astype(BF16), upper)
        if c_in is not None:
            e = e - c_in
        a = jnp.exp2(e)
        if masked:
            a = jnp.where(causal, a, 0.0)
        a = a.astype(BF16)
        a_cat = jnp.concatenate([a[:tt], a[tt:]], axis=1)
        return _dot(a_cat, v_bd), jnp.sum(lp, axis=1, keepdims=True)

    kprev = jnp.maximum(qi - 1, 0)
    penalty = jnp.where(qi > 0, 0.0, 1e30).astype(F32)
    mn = None
    for pr in range(SB_PAIRS):
        o_d, tot_d = block(pr, qi, None, True)
        c1 = tot_d + penalty
        o_p, tot_p = block(pr, kprev, c1, False)
        acc_ref[pr] = o_d + o_p
        c2 = c1 + tot_p
        c_ref[pr] = c2
        m = jnp.min(c2)
        mn = m if mn is None else jnp.minimum(mn, m)

    def cond(s):
        kb, mn = s
        return jnp.logical_and(kb >= 0, mn < EXP2_UNDERFLOW)

    def body(s):
        kb, _ = s
        mn = None
        for pr in range(SB_PAIRS):
            c = c_ref[pr]
            o, tot = block(pr, kb, c, False)
            acc_ref[pr] += o
            c = c + tot
            c_ref[pr] = c
            m = jnp.min(c)
            mn = m if mn is None else jnp.minimum(mn, m)
        return kb - 1, mn

    lax.while_loop(cond, body, (qi - 2, mn))
    o_ref[0] = jnp.concatenate([acc_ref[pr] for pr in range(SB_PAIRS)], axis=1).astype(o_ref.dtype)


def _attention(q, k, v):
    b, s, w = q.shape
    tt = SB_TILE
    bw = SB_PAIRS * LANES
    return pl.pallas_call(
        _attn_body,
        grid=(b, w // bw, s // tt),
        in_specs=[
            pl.BlockSpec((1, tt, bw), lambda bi, hp, qi: (bi, qi, hp)),
            pl.BlockSpec((1, s, bw), lambda bi, hp, qi: (bi, 0, hp)),
            pl.BlockSpec((1, s, bw), lambda bi, hp, qi: (bi, 0, hp)),
        ],
        out_specs=pl.BlockSpec((1, tt, bw), lambda bi, hp, qi: (bi, qi, hp)),
        out_shape=jax.ShapeDtypeStruct((b, s, w), BF16),
        scratch_shapes=[pltpu.VMEM((SB_PAIRS, tt, LANES), F32),
                        pltpu.VMEM((SB_PAIRS, 2 * tt, 1), F32)],
        compiler_params=_params("parallel", "parallel", "arbitrary"),
        name="sb_attn",
    )(q, k, v)


def _ssd_body(sw, xbc_ref, z_ref, dt_ref, cw_ref, cb_ref, dtb_ref, alog_ref, dskip_ref, ng_ref,
              o_ref, xpad_ref, st_ref):
    ll = SSD_CHUNK
    gn = SSD_GROUPS * SSD_STATE
    gw = sw // SSD_GROUPS
    pad = SUBLANES
    c = pl.program_id(1)

    @pl.when(c == 0)
    def _():
        xpad_ref[0:pad, :] = jnp.zeros((pad, xpad_ref.shape[1]), F32)
        st_ref[...] = jnp.zeros_like(st_ref)

    xpad_ref[pad:pad + ll, :] = xbc_ref[0]
    conv = cb_ref[...]
    for k in range(SSD_CONV_K):
        off = pad - (SSD_CONV_K - 1) + k
        conv = conv + cw_ref[k:k + 1, :] * xpad_ref[off:off + ll, :]
    xpad_ref[0:pad, :] = xpad_ref[ll:ll + pad, :]
    xbc = _silu(conv)
    xs = xbc[:, 0:sw]
    bm = xbc[:, sw:sw + gn]
    cm = xbc[:, sw + gn:sw + 2 * gn]

    dtr = dt_ref[0] + dtb_ref[...]
    dt = jnp.maximum(dtr, 0.0) + jnp.log1p(jnp.exp(-jnp.abs(dtr)))
    adt = dt * (-jnp.exp(alog_ref[...]))
    row = lax.broadcasted_iota(jnp.int32, (ll, ll), 0)
    col = lax.broadcasted_iota(jnp.int32, (ll, ll), 1)
    tri = col <= row
    a_cs = _dot_split(adt, jnp.where(tri, 1.0, 0.0).astype(BF16), 3, lhs=False)
    a_cs_t = a_cs.T
    a_last = a_cs[ll - 1:ll, :]
    er = lax.broadcasted_iota(jnp.int32, (LANES, sw), 0)
    ec = lax.broadcasted_iota(jnp.int32, (LANES, sw), 1)
    expand = jnp.where(ec // SSD_HEAD_DIM == er, 1.0, 0.0).astype(BF16)
    fac = jnp.concatenate([dt, jnp.exp(a_last - a_cs), jnp.exp(a_cs)], axis=0)
    fac_e = _dot_split(fac, expand, 2)
    dt_e, dec_e, ea_e = fac_e[0:ll], fac_e[ll:2 * ll], fac_e[2 * ll:3 * ll]

    xdt = xs * dt_e
    xdt_b = xdt.astype(BF16)
    xdec_b = (xdt * dec_e).astype(BF16)
    lane = lax.broadcasted_iota(jnp.int32, (ll, LANES), 1)
    heads_per_group = gw // SSD_HEAD_DIM
    ys = []
    for g in range(SSD_GROUPS):
        bg_t = bm[:, g * SSD_STATE:(g + 1) * SSD_STATE].T.astype(BF16)
        cg = cm[:, g * SSD_STATE:(g + 1) * SSD_STATE].astype(BF16)
        cb = _dot(cg, bg_t)
        pairs = []
        for p in range(heads_per_group // 2):
            res = []
            for e in range(2):
                hd = g * heads_per_group + 2 * p + e
                diff = a_cs[:, hd:hd + 1] - a_cs_t[hd:hd + 1, :]
                lm = jnp.exp(jnp.where(tri, diff, -jnp.inf))
                blk = (g * heads_per_group + 2 * p) * SSD_HEAD_DIM
                res.append(_dot((cb * lm).astype(BF16), xdt_b[:, blk:blk + LANES]))
            pairs.append(jnp.where(lane < SSD_HEAD_DIM, res[0], res[1]))
        y_diag = jnp.concatenate(pairs, axis=1)
        sl = slice(g * gw, (g + 1) * gw)
        st = st_ref[g]
        y_off = _dot(cg, st.astype(BF16)) * ea_e[:, sl]
        st_ref[g] = st * ea_e[ll - 1:ll, sl] + _dot(bg_t, xdec_b[:, sl])
        ys.append(y_diag + y_off)
    y = jnp.concatenate(ys, axis=1) + dskip_ref[...] * xs
    y = y * _silu(z_ref[0].astype(F32))
    outs = [_rms(y[:, g * gw:(g + 1) * gw]) for g in range(SSD_GROUPS)]
    o_ref[0] = (jnp.concatenate(outs, axis=1) * ng_ref[...]).astype(o_ref.dtype)


def _ssd(xbc, z, dt_raw, conv_w, conv_b, dt_bias, a_log, d_skip, norm_g):
    b, s, ch = xbc.shape
    sw = z.shape[2]
    nh = sw // SSD_HEAD_DIM
    ll = SSD_CHUNK

    def head_row(v):
        return jnp.zeros((1, LANES), F32).at[0, :nh].set(v)

    row_spec = lambda n: pl.BlockSpec((1, n), lambda bi, ci: (0, 0))
    return pl.pallas_call(
        functools.partial(_ssd_body, sw),
        grid=(b, s // ll),
        in_specs=[
            pl.BlockSpec((1, ll, ch), lambda bi, ci: (bi, ci, 0)),
            pl.BlockSpec((1, ll, sw), lambda bi, ci: (bi, ci, 0)),
            pl.BlockSpec((1, ll, LANES), lambda bi, ci: (bi, ci, 0)),
            pl.BlockSpec((SSD_CONV_K, ch), lambda bi, ci: (0, 0)),
            row_spec(ch), row_spec(LANES), row_spec(LANES), row_spec(sw), row_spec(sw),
        ],
        out_specs=pl.BlockSpec((1, ll, sw), lambda bi, ci: (bi, ci, 0)),
        out_shape=jax.ShapeDtypeStruct((b, s, sw), BF16),
        scratch_shapes=[pltpu.VMEM((ll + SUBLANES, ch), F32),
                        pltpu.VMEM((SSD_GROUPS, SSD_STATE, sw // SSD_GROUPS), F32)],
        compiler_params=_params("parallel", "arbitrary"),
        name="ssd",
    )(xbc, z, dt_raw, conv_w, conv_b.reshape(1, ch), head_row(dt_bias), head_row(a_log),
      jnp.repeat(d_skip, SSD_HEAD_DIM).reshape(1, sw), norm_g.reshape(1, sw))


def _outproj_body(oa_ref, os_ref, h_ref, ag_ref, wa_ref, ws_ref, pg_ref, o_ref):
    oan = (_rms(oa_ref[...].astype(F32)) * ag_ref[...]).astype(BF16)
    m = _dot(oan, wa_ref[...]) + _dot(os_ref[...], ws_ref[...])
    o_ref[...] = h_ref[...] + _rms(m) * pg_ref[...]


def _outproj(o_att, o_ssd, h, attn_g, w_out, post_g, *, tm=1024):
    t, d = h.shape
    aw, sw = o_att.shape[1], o_ssd.shape[1]
    assert aw == sw
    return pl.pallas_call(
        _outproj_body,
        grid=(t // tm,),
        in_specs=[
            pl.BlockSpec((tm, aw), lambda i: (i, 0)),
            pl.BlockSpec((tm, sw), lambda i: (i, 0)),
            pl.BlockSpec((tm, d), lambda i: (i, 0)),
            pl.BlockSpec((1, aw), lambda i: (0, 0)),
            pl.BlockSpec((aw, d), lambda i: (0, 0)),
            pl.BlockSpec((sw, d), lambda i: (1, 0)),
            pl.BlockSpec((1, d), lambda i: (0, 0)),
        ],
        out_specs=pl.BlockSpec((tm, d), lambda i: (i, 0)),
        out_shape=jax.ShapeDtypeStruct((t, d), F32),
        compiler_params=_params("parallel"),
        name="outproj",
    )(o_att, o_ssd, h, attn_g.reshape(1, aw), w_out, w_out, post_g.reshape(1, d))


def _memkv_body(mem_ref, g_ref, w_ref, kv_ref):
    kv_ref[0] = _dot((_rms(mem_ref[0]) * g_ref[...]).astype(BF16), w_ref[...]).astype(BF16)


def _memkv(mem, g, wkv):
    b, m, d = mem.shape
    return pl.pallas_call(
        _memkv_body,
        grid=(b,),
        in_specs=[
            pl.BlockSpec((1, m, d), lambda bi: (bi, 0, 0)),
            pl.BlockSpec((1, d), lambda bi: (0, 0)),
            pl.BlockSpec((d, 2 * d), lambda bi: (0, 0)),
        ],
        out_specs=pl.BlockSpec((1, m, 2 * d), lambda bi: (bi, 0, 0)),
        out_shape=jax.ShapeDtypeStruct((b, m, 2 * d), BF16),
        compiler_params=_params("parallel"),
        name="memkv",
    )(mem, g.reshape(1, d), wkv)


def _xattn_body(h_ref, kv_ref, g_ref, wq_ref, wo_ref, pg_ref, o_ref):
    d = h_ref.shape[2]
    hd = d // XA_HEADS
    h = h_ref[0]
    hn = (_rms(h) * g_ref[...]).astype(BF16)
    q = (_dot(hn, wq_ref[...]) * (1.0 / hd ** 0.5)).astype(BF16)
    outs = []
    for i in range(XA_HEADS):
        sc = _dot_nt(q[:, i * hd:(i + 1) * hd], kv_ref[0, :, i * hd:(i + 1) * hd])
        e = jnp.exp(sc - jnp.max(sc, axis=-1, keepdims=True))
        p = e / jnp.sum(e, axis=-1, keepdims=True)
        outs.append(_dot(p.astype(BF16), kv_ref[0, :, d + i * hd:d + (i + 1) * hd]).astype(BF16))
    c = _dot(jnp.concatenate(outs, axis=1), wo_ref[...])
    o_ref[0] = h + _rms(c) * pg_ref[...]


def _xattn(h, kv, pre_g, wq, wo, post_g, *, tm=1024):
    b, s, d = h.shape
    m = kv.shape[1]
    return pl.pallas_call(
        _xattn_body,
        grid=(b, s // tm),
        in_specs=[
            pl.BlockSpec((1, tm, d), lambda bi, i: (bi, i, 0)),
            pl.BlockSpec((1, m, 2 * d), lambda bi, i: (bi, 0, 0)),
            pl.BlockSpec((1, d), lambda bi, i: (0, 0)),
            pl.BlockSpec((d, d), lambda bi, i: (0, 0)),
            pl.BlockSpec((d, d), lambda bi, i: (0, 0)),
            pl.BlockSpec((1, d), lambda bi, i: (0, 0)),
        ],
        out_specs=pl.BlockSpec((1, tm, d), lambda bi, i: (bi, i, 0)),
        out_shape=jax.ShapeDtypeStruct((b, s, d), F32),
        compiler_params=_params("parallel", "parallel"),
        name="xattn",
    )(h, kv, pre_g.reshape(1, d), wq, wo, post_g.reshape(1, d))


def _mixer(h, b, s, mix_pre_g, w_in, conv_w, conv_b, dt_bias, a_log, d_skip, ssd_norm_g,
           attn_norm_g, w_out, mix_post_g):
    d = h.shape[1]
    nh = dt_bias.shape[0]
    sw = nh * SSD_HEAD_DIM
    cw = conv_w.shape[1]
    aw = (w_in.shape[1] - nh - sw - cw) // 3
    n_main = 3 * aw + sw + cw
    w_main = w_in[:, :n_main].astype(BF16)
    w_dt = jnp.zeros((d, LANES), BF16).at[:, :nh].set(w_in[:, n_main:].astype(BF16))
    q, k, v, z, xbc, dt_raw = _inproj(h, mix_pre_g, w_main, w_dt, aw, sw, cw)
    o_att = _attention(q.reshape(b, s, aw), k.reshape(b, s, aw), v.reshape(b, s, aw))
    o_ssd = _ssd(xbc.reshape(b, s, cw), z.reshape(b, s, sw), dt_raw.reshape(b, s, LANES),
                 conv_w, conv_b, dt_bias, a_log, d_skip, ssd_norm_g)
    return _outproj(o_att.reshape(b * s, aw), o_ssd.reshape(b * s, sw), h, attn_norm_g,
                    w_out.astype(BF16), mix_post_g)


def kernel(x, mem, ffn1_pre_g, ffn1_w_gu, ffn1_w_down, ffn1_post_g, mix_pre_g, w_in, conv_w, conv_b, dt_bias, a_log, d_skip, ssd_norm_g, attn_norm_g, w_out, mix_post_g, xa_pre_g, mem_g, xa_wq, xa_wkv, xa_wo, xa_post_g, ffn2_pre_g, ffn2_w_gu, ffn2_w_down, ffn2_post_g):
    b, s, d = x.shape
    h = x.reshape(b * s, d)
    for l in range(w_in.shape[0]):
        h = _ffn(h, ffn1_pre_g[l], ffn1_w_gu[l].astype(BF16), ffn1_w_down[l].astype(BF16),
                 ffn1_post_g[l])
        h = _mixer(h, b, s, mix_pre_g[l], w_in[l], conv_w[l], conv_b[l], dt_bias[l], a_log[l],
                   d_skip[l], ssd_norm_g[l], attn_norm_g[l], w_out[l], mix_post_g[l])
        kv = _memkv(mem, mem_g[l], xa_wkv[l].astype(BF16))
        h = _xattn(h.reshape(b, s, d), kv, xa_pre_g[l], xa_wq[l].astype(BF16),
                   xa_wo[l].astype(BF16), xa_post_g[l]).reshape(b * s, d)
        h = _ffn(h, ffn2_pre_g[l], ffn2_w_gu[l].astype(BF16), ffn2_w_down[l].astype(BF16),
                 ffn2_post_g[l])
    return h.reshape(b, s, d)
```

```python
import functools

import jax
import jax.numpy as jnp
from jax import lax
from jax.experimental import pallas as pl
from jax.experimental.pallas import tpu as pltpu

F32 = jnp.float32
BF16 = jnp.bfloat16

EPS = 1e-6
LANES = 128
SUBLANES = 8
VMEM_LIMIT_BYTES = 56 * 1024 * 1024

SB_HEAD_DIM = 64
SB_TILE = 256
SB_PAIRS = 4
LOG2_E = 1.4426950408889634
EXP2_UNDERFLOW = 150.0
EXP2_CLAMP = 126.0

SSD_HEAD_DIM = 64
SSD_GROUPS = 2
SSD_STATE = 128
SSD_CONV_K = 4
SSD_CHUNK = 128

XA_HEADS = 4


def _params(*sem):
    return pltpu.CompilerParams(dimension_semantics=sem, vmem_limit_bytes=VMEM_LIMIT_BYTES)


def _rms(x):
    return x * lax.rsqrt(jnp.mean(x * x, axis=-1, keepdims=True) + EPS)


def _dot(a, b):
    return jnp.dot(a, b, preferred_element_type=F32)


def _dot_nt(a, b):
    return lax.dot_general(a, b, (((1,), (1,)), ((), ())), preferred_element_type=F32)


def _silu(x):
    return x * jax.nn.sigmoid(x)


def _split_bf16(x, parts):
    out = []
    r = x
    for _ in range(parts - 1):
        p = r.astype(BF16)
        out.append(p)
        r = r - p.astype(F32)
    out.append(r.astype(BF16))
    return out


def _dot_split(x, w_bf16, parts, lhs=True):
    acc = None
    for p in _split_bf16(x, parts):
        t = _dot(p, w_bf16) if lhs else _dot(w_bf16, p)
        acc = t if acc is None else acc + t
    return acc


def _ffn_body(h_ref, pre_ref, wgu_ref, wd_ref, post_ref, o_ref):
    d_ff = wd_ref.shape[0]
    h = h_ref[...]
    hn = (_rms(h) * pre_ref[...]).astype(BF16)
    g = _dot(hn, wgu_ref[:, :d_ff])
    u = _dot(hn, wgu_ref[:, d_ff:])
    f = _dot((_silu(g) * u).astype(BF16), wd_ref[...])
    o_ref[...] = h + 0.5 * (_rms(f) * post_ref[...])


def _ffn(h, pre_g, w_gu, w_down, post_g, *, tm=512):
    t, d = h.shape
    d_ff = w_down.shape[0]
    return pl.pallas_call(
        _ffn_body,
        grid=(t // tm,),
        in_specs=[
            pl.BlockSpec((tm, d), lambda i: (i, 0)),
            pl.BlockSpec((1, d), lambda i: (0, 0)),
            pl.BlockSpec((d, 2 * d_ff), lambda i: (0, 0)),
            pl.BlockSpec((d_ff, d), lambda i: (0, 0)),
            pl.BlockSpec((1, d), lambda i: (0, 0)),
        ],
        out_specs=pl.BlockSpec((tm, d), lambda i: (i, 0)),
        out_shape=jax.ShapeDtypeStruct((t, d), F32),
        compiler_params=_params("parallel"),
        name="ffn",
    )(h, pre_g.reshape(1, d), w_gu, w_down, post_g.reshape(1, d))


def _inproj_body(aw, h_ref, g_ref, w_ref, q_ref, k_ref, v_ref, hn_ref):
    hn = (_rms(h_ref[...]) * g_ref[...]).astype(BF16)
    hn_ref[...] = hn
    scale = LOG2_E / (SB_HEAD_DIM ** 0.5)
    q_ref[...] = (_dot(hn, w_ref[:, 0:aw]) * scale).astype(BF16)
    k_ref[...] = _dot(hn, w_ref[:, aw:2 * aw]).astype(BF16)
    v_ref[...] = _dot(hn, w_ref[:, 2 * aw:3 * aw]).astype(BF16)


def _inproj(h, g, w_qkv, aw, *, tm=512):
    t, d = h.shape
    outs = [aw, aw, aw, d]
    return pl.pallas_call(
        functools.partial(_inproj_body, aw),
        grid=(t // tm,),
        in_specs=[
            pl.BlockSpec((tm, d), lambda i: (i, 0)),
            pl.BlockSpec((1, d), lambda i: (0, 0)),
            pl.BlockSpec((d, 3 * aw), lambda i: (0, 0)),
        ],
        out_specs=[pl.BlockSpec((tm, n), lambda i: (i, 0)) for n in outs],
        out_shape=[jax.ShapeDtypeStruct((t, n), BF16) for n in outs],
        compiler_params=_params("parallel"),
        name="inproj",
    )(h, g.reshape(1, d), w_qkv)


def _attn_body(q_ref, k_ref, v_ref, o_ref, acc_ref, c_ref):
    tt = SB_TILE
    qi = pl.program_id(2)
    lane = lax.broadcasted_iota(jnp.int32, (tt, LANES), 1)
    first = lane < SB_HEAD_DIM
    q_pairs = []
    for pr in range(SB_PAIRS):
        q = q_ref[0, :, pr * LANES:(pr + 1) * LANES]
        zero = jnp.zeros_like(q)
        q_pairs.append(jnp.concatenate([jnp.where(first, q, zero), jnp.where(first, zero, q)], axis=0))
    row = lax.broadcasted_iota(jnp.int32, (2 * tt, tt), 0)
    col = lax.broadcasted_iota(jnp.int32, (2 * tt, tt), 1)
    causal = col < jnp.where(row >= tt, row - tt, row)
    urow = lax.broadcasted_iota(jnp.int32, (tt, tt), 0)
    ucol = lax.broadcasted_iota(jnp.int32, (tt, tt), 1)
    upper = jnp.where(urow > ucol, 1.0, 0.0).astype(BF16)

    def block(pr, kb, c_in, masked):
        start = pl.multiple_of(kb * tt, tt)
        sl = pl.ds(pr * LANES, LANES)
        kblk = k_ref[0, pl.ds(start, tt), sl]
        vblk = v_ref[0, pl.ds(start, tt), sl]
        zero = jnp.zeros_like(vblk)
        v_bd = jnp.concatenate([jnp.where(first, vblk, zero), jnp.where(first, zero, vblk)], axis=0)
        z = jnp.minimum(_dot_nt(q_pairs[pr], kblk), EXP2_CLAMP)
        lp = jnp.log2(1.0 + jnp.exp2(z))
        if masked:
            lp = jnp.where(causal, lp, 0.0)
        e = z - lp - _dot(lp.astype(BF16), upper)
        if c_in is not None:
            e = e - c_in
        a = jnp.exp2(e)
        if masked:
            a = jnp.where(causal, a, 0.0)
        a = a.astype(BF16)
        a_cat = jnp.concatenate([a[:tt], a[tt:]], axis=1)
        return _dot(a_cat, v_bd), jnp.sum(lp, axis=1, keepdims=True)

    kprev = jnp.maximum(qi - 1, 0)
    penalty = jnp.where(qi > 0, 0.0, 1e30).astype(F32)
    mn = None
    for pr in range(SB_PAIRS):
        o_d, tot_d = block(pr, qi, None, True)
        c1 = tot_d + penalty
        o_p, tot_p = block(pr, kprev, c1, False)
        acc_ref[pr] = o_d + o_p
        c2 = c1 + tot_p
        c_ref[pr] = c2
        m = jnp.min(c2)
        mn = m if mn is None else jnp.minimum(mn, m)

    def cond(s):
        kb, mn = s
        return jnp.logical_and(kb >= 0, mn < EXP2_UNDERFLOW)

    def body(s):
        kb, _ = s
        mn = None
        for pr in range(SB_PAIRS):
            c = c_ref[pr]
            o, tot = block(pr, kb, c, False)
            acc_ref[pr] += o
            c = c + tot
            c_ref[pr] = c
            m = jnp.min(c)
            mn = m if mn is None else jnp.minimum(mn, m)
        return kb - 1, mn

    lax.while_loop(cond, body, (qi - 2, mn))
    o_ref[0] = jnp.concatenate([acc_ref[pr] for pr in range(SB_PAIRS)], axis=1).astype(o_ref.dtype)


def _attention(q, k, v):
    b, s, w = q.shape
    tt = SB_TILE
    bw = SB_PAIRS * LANES
    return pl.pallas_call(
        _attn_body,
        grid=(b, w // bw, s // tt),
        in_specs=[
            pl.BlockSpec((1, tt, bw), lambda bi, hp, qi: (bi, qi, hp)),
            pl.BlockSpec((1, s, bw), lambda bi, hp, qi: (bi, 0, hp)),
            pl.BlockSpec((1, s, bw), lambda bi, hp, qi: (bi, 0, hp)),
        ],
        out_specs=pl.BlockSpec((1, tt, bw), lambda bi, hp, qi: (bi, qi, hp)),
        out_shape=jax.ShapeDtypeStruct((b, s, w), BF16),
        scratch_shapes=[pltpu.VMEM((SB_PAIRS, tt, LANES), F32),
                        pltpu.VMEM((SB_PAIRS, 2 * tt, 1), F32)],
        compiler_params=_params("parallel", "parallel", "arbitrary"),
        name="sb_attn",
    )(q, k, v)


def _ssd_body(sw, hn_ref, hn_next_ref, w_ref, cw_ref, cb_ref, dtb_ref, alog_ref, dskip_ref, ng_ref,
              o_ref, xpad_ref, st_ref, pa_ref, pb_ref):
    ll = SSD_CHUNK
    gn = SSD_GROUPS * SSD_STATE
    gw = sw // SSD_GROUPS
    pad = SUBLANES
    ch = xpad_ref.shape[1]
    c = pl.program_id(1)
    row = lax.broadcasted_iota(jnp.int32, (ll, ll), 0)
    col = lax.broadcasted_iota(jnp.int32, (ll, ll), 1)
    tri = col <= row
    tri_b = jnp.where(tri, 1.0, 0.0).astype(BF16)
    er = lax.broadcasted_iota(jnp.int32, (LANES, sw), 0)
    ec = lax.broadcasted_iota(jnp.int32, (LANES, sw), 1)
    expand = jnp.where(ec // SSD_HEAD_DIM == er, 1.0, 0.0).astype(BF16)
    lane = lax.broadcasted_iota(jnp.int32, (ll, LANES), 1)

    @pl.when(c == 0)
    def _():
        xpad_ref[0:pad, :] = jnp.zeros((pad, ch), F32)
        st_ref[...] = jnp.zeros_like(st_ref)
        pa_ref[...] = _dot(hn_ref[0, 0:ll, :], w_ref[...])

    def chunk(p_ref, half):
        xpad_ref[pad:pad + ll, :] = p_ref[:, sw:sw + ch]
        xall = xpad_ref[...]
        acc = cw_ref[0:1, :] * xall
        for k in range(1, SSD_CONV_K):
            acc = pltpu.roll(acc, 1, axis=0) + cw_ref[k:k + 1, :] * xall
        conv = acc[pad:pad + ll, :] + cb_ref[...]
        xpad_ref[0:pad, :] = xpad_ref[ll:ll + pad, :]
        xbc = _silu(conv)
        xs = xbc[:, 0:sw]
        bm = xbc[:, sw:sw + gn]
        cm = xbc[:, sw + gn:sw + 2 * gn]

        dtr = p_ref[:, sw + ch:] + dtb_ref[...]
        dt = jnp.maximum(dtr, 0.0) + jnp.log1p(jnp.exp(-jnp.abs(dtr)))
        adt = dt * (-jnp.exp(alog_ref[...]))
        a_cs = _dot_split(adt, tri_b, 3, lhs=False)
        a_cs_t = a_cs.T
        a_last = a_cs[ll - 1:ll, :]
        fac = jnp.concatenate([dt, jnp.exp(a_last - a_cs)], axis=0)
        fac_e = _dot(fac.astype(BF16), expand)
        dt_e, dec_e = fac_e[0:ll], fac_e[ll:2 * ll]
        ea_e = _dot_split(jnp.exp(a_cs), expand, 2)

        xdt = xs * dt_e
        xdt_b = xdt.astype(BF16)
        xdec_b = (xdt * dec_e).astype(BF16)
        heads_per_group = gw // SSD_HEAD_DIM
        ys = []
        for g in range(SSD_GROUPS):
            bg_t = bm[:, g * SSD_STATE:(g + 1) * SSD_STATE].T.astype(BF16)
            cg = cm[:, g * SSD_STATE:(g + 1) * SSD_STATE].astype(BF16)
            cb = _dot(cg, bg_t)
            pairs = []
            for p in range(heads_per_group // 2):
                res = []
                for e in range(2):
                    hd = g * heads_per_group + 2 * p + e
                    diff = a_cs[:, hd:hd + 1] - a_cs_t[hd:hd + 1, :]
                    lm = jnp.exp(jnp.where(tri, diff, -jnp.inf))
                    blk = (g * heads_per_group + 2 * p) * SSD_HEAD_DIM
                    res.append(_dot((cb * lm).astype(BF16), xdt_b[:, blk:blk + LANES]))
                pairs.append(jnp.where(lane < SSD_HEAD_DIM, res[0], res[1]))
            y_diag = jnp.concatenate(pairs, axis=1)
            sl = slice(g * gw, (g + 1) * gw)
            st = st_ref[g]
            y_off = _dot(cg, st.astype(BF16)) * ea_e[:, sl]
            st_ref[g] = st * ea_e[ll - 1:ll, sl] + _dot(bg_t, xdec_b[:, sl])
            ys.append(y_diag + y_off)
        y = jnp.concatenate(ys, axis=1) + dskip_ref[...] * xs
        y = y * _silu(p_ref[:, 0:sw])
        outs = [_rms(y[:, g * gw:(g + 1) * gw]) for g in range(SSD_GROUPS)]
        o_ref[0, half * ll:(half + 1) * ll, :] = (
            jnp.concatenate(outs, axis=1) * ng_ref[...]).astype(o_ref.dtype)

    pb_ref[...] = _dot(hn_ref[0, ll:2 * ll, :], w_ref[...])
    chunk(pa_ref, 0)
    pa_ref[...] = _dot(hn_next_ref[0], w_ref[...])
    chunk(pb_ref, 1)


def _ssd(hn, w_ssd, conv_w, conv_b, dt_bias, a_log, d_skip, norm_g):
    b, s, d = hn.shape
    ch = conv_w.shape[1]
    nh = dt_bias.shape[0]
    sw = nh * SSD_HEAD_DIM
    ll = SSD_CHUNK

    def head_row(v):
        return jnp.zeros((1, LANES), F32).at[0, :nh].set(v)

    row_spec = lambda n: pl.BlockSpec((1, n), lambda bi, ci: (0, 0))
    return pl.pallas_call(
        functools.partial(_ssd_body, sw),
        grid=(b, s // (2 * ll)),
        in_specs=[
            pl.BlockSpec((1, 2 * ll, d), lambda bi, ci: (bi, ci, 0)),
            pl.BlockSpec((1, ll, d), lambda bi, ci: (bi, jnp.minimum(2 * ci + 2, s // ll - 1), 0)),
            pl.BlockSpec(w_ssd.shape, lambda bi, ci: (0, 0)),
            pl.BlockSpec((SSD_CONV_K, ch), lambda bi, ci: (0, 0)),
            row_spec(ch), row_spec(LANES), row_spec(LANES), row_spec(sw), row_spec(sw),
        ],
        out_specs=pl.BlockSpec((1, 2 * ll, sw), lambda bi, ci: (bi, ci, 0)),
        out_shape=jax.ShapeDtypeStruct((b, s, sw), BF16),
        scratch_shapes=[pltpu.VMEM((ll + SUBLANES, ch), F32),
                        pltpu.VMEM((SSD_GROUPS, SSD_STATE, sw // SSD_GROUPS), F32),
                        pltpu.VMEM((ll, w_ssd.shape[1]), F32),
                        pltpu.VMEM((ll, w_ssd.shape[1]), F32)],
        compiler_params=_params("parallel", "arbitrary"),
        name="ssd",
    )(hn, hn, w_ssd, conv_w, conv_b.reshape(1, ch), head_row(dt_bias), head_row(a_log),
      jnp.repeat(d_skip, SSD_HEAD_DIM).reshape(1, sw), norm_g.reshape(1, sw))


def _outproj_body(oa_ref, os_ref, h_ref, ag_ref, wa_ref, ws_ref, pg_ref, o_ref):
    oan = (_rms(oa_ref[...].astype(F32)) * ag_ref[...]).astype(BF16)
    m = _dot(oan, wa_ref[...]) + _dot(os_ref[...], ws_ref[...])
    o_ref[...] = h_ref[...] + _rms(m) * pg_ref[...]


def _outproj(o_att, o_ssd, h, attn_g, w_out, post_g, *, tm=1024):
    t, d = h.shape
    aw, sw = o_att.shape[1], o_ssd.shape[1]
    assert aw == sw
    return pl.pallas_call(
        _outproj_body,
        grid=(t // tm,),
        in_specs=[
            pl.BlockSpec((tm, aw), lambda i: (i, 0)),
            pl.BlockSpec((tm, sw), lambda i: (i, 0)),
            pl.BlockSpec((tm, d), lambda i: (i, 0)),
            pl.BlockSpec((1, aw), lambda i: (0, 0)),
            pl.BlockSpec((aw, d), lambda i: (0, 0)),
            pl.BlockSpec((sw, d), lambda i: (1, 0)),
            pl.BlockSpec((1, d), lambda i: (0, 0)),
        ],
        out_specs=pl.BlockSpec((tm, d), lambda i: (i, 0)),
        out_shape=jax.ShapeDtypeStruct((t, d), F32),
        compiler_params=_params("parallel"),
        name="outproj",
    )(o_att, o_ssd, h, attn_g.reshape(1, aw), w_out, w_out, post_g.reshape(1, d))


def _memkv_body(mem_ref, g_ref, w_ref, kv_ref):
    kv_ref[0] = _dot((_rms(mem_ref[0]) * g_ref[...]).astype(BF16), w_ref[...]).astype(BF16)


def _memkv(mem, g, wkv):
    b, m, d = mem.shape
    return pl.pallas_call(
        _memkv_body,
        grid=(b,),
        in_specs=[
            pl.BlockSpec((1, m, d), lambda bi: (bi, 0, 0)),
            pl.BlockSpec((1, d), lambda bi: (0, 0)),
            pl.BlockSpec((d, 2 * d), lambda bi: (0, 0)),
        ],
        out_specs=pl.BlockSpec((1, m, 2 * d), lambda bi: (bi, 0, 0)),
        out_shape=jax.ShapeDtypeStruct((b, m, 2 * d), BF16),
        compiler_params=_params("parallel"),
        name="memkv",
    )(mem, g.reshape(1, d), wkv)


def _xattn_body(h_ref, kv_ref, g_ref, wq_ref, wo_ref, pg_ref, o_ref):
    d = h_ref.shape[2]
    hd = d // XA_HEADS
    h = h_ref[0]
    hn = (_rms(h) * g_ref[...]).astype(BF16)
    q = (_dot(hn, wq_ref[...]) * (1.0 / hd ** 0.5)).astype(BF16)
    outs = []
    for i in range(XA_HEADS):
        sc = _dot_nt(q[:, i * hd:(i + 1) * hd], kv_ref[0, :, i * hd:(i + 1) * hd])
        e = jnp.exp(sc - jnp.max(sc, axis=-1, keepdims=True))
        p = e / jnp.sum(e, axis=-1, keepdims=True)
        outs.append(_dot(p.astype(BF16), kv_ref[0, :, d + i * hd:d + (i + 1) * hd]).astype(BF16))
    c = _dot(jnp.concatenate(outs, axis=1), wo_ref[...])
    o_ref[0] = h + _rms(c) * pg_ref[...]


def _xattn(h, kv, pre_g, wq, wo, post_g, *, tm=1024):
    b, s, d = h.shape
    m = kv.shape[1]
    return pl.pallas_call(
        _xattn_body,
        grid=(b, s // tm),
        in_specs=[
            pl.BlockSpec((1, tm, d), lambda bi, i: (bi, i, 0)),
            pl.BlockSpec((1, m, 2 * d), lambda bi, i: (bi, 0, 0)),
            pl.BlockSpec((1, d), lambda bi, i: (0, 0)),
            pl.BlockSpec((d, d), lambda bi, i: (0, 0)),
            pl.BlockSpec((d, d), lambda bi, i: (0, 0)),
            pl.BlockSpec((1, d), lambda bi, i: (0, 0)),
        ],
        out_specs=pl.BlockSpec((1, tm, d), lambda bi, i: (bi, i, 0)),
        out_shape=jax.ShapeDtypeStruct((b, s, d), F32),
        compiler_params=_params("parallel", "parallel"),
        name="xattn",
    )(h, kv, pre_g.reshape(1, d), wq, wo, post_g.reshape(1, d))


def _mixer(h, b, s, mix_pre_g, w_in, conv_w, conv_b, dt_bias, a_log, d_skip, ssd_norm_g,
           attn_norm_g, w_out, mix_post_g):
    d = h.shape[1]
    nh = dt_bias.shape[0]
    sw = nh * SSD_HEAD_DIM
    cw = conv_w.shape[1]
    aw = (w_in.shape[1] - nh - sw - cw) // 3
    w_bf = w_in.astype(BF16)
    w_ssd = jnp.concatenate([w_bf[:, 3 * aw:], jnp.zeros((d, LANES - nh), BF16)], axis=1)
    q, k, v, hn = _inproj(h, mix_pre_g, w_bf[:, :3 * aw], aw)
    o_att = _attention(q.reshape(b, s, aw), k.reshape(b, s, aw), v.reshape(b, s, aw))
    o_ssd = _ssd(hn.reshape(b, s, d), w_ssd, conv_w, conv_b, dt_bias, a_log, d_skip, ssd_norm_g)
    return _outproj(o_att.reshape(b * s, aw), o_ssd.reshape(b * s, sw), h, attn_norm_g,
                    w_out.astype(BF16), mix_post_g)


def kernel(x, mem, ffn1_pre_g, ffn1_w_gu, ffn1_w_down, ffn1_post_g, mix_pre_g, w_in, conv_w, conv_b, dt_bias, a_log, d_skip, ssd_norm_g, attn_norm_g, w_out, mix_post_g, xa_pre_g, mem_g, xa_wq, xa_wkv, xa_wo, xa_post_g, ffn2_pre_g, ffn2_w_gu, ffn2_w_down, ffn2_post_g):
    b, s, d = x.shape
    h = x.reshape(b * s, d)
    for l in range(w_in.shape[0]):
        h = _ffn(h, ffn1_pre_g[l], ffn1_w_gu[l].astype(BF16), ffn1_w_down[l].astype(BF16),
                 ffn1_post_g[l])
        h = _mixer(h, b, s, mix_pre_g[l], w_in[l], conv_w[l], conv_b[l], dt_bias[l], a_log[l],
                   d_skip[l], ssd_norm_g[l], attn_norm_g[l], w_out[l], mix_post_g[l])
        kv = _memkv(mem, mem_g[l], xa_wkv[l].astype(BF16))
        h = _xattn(h.reshape(b, s, d), kv, xa_pre_g[l], xa_wq[l].astype(BF16),
                   xa_wo[l].astype(BF16), xa_post_g[l]).reshape(b * s, d)
        h = _ffn(h, ffn2_pre_g[l], ffn2_w_gu[l].astype(BF16), ffn2_w_down[l].astype(BF16),
                 ffn2_post_g[l])
    return h.reshape(b, s, d)
```

```python
import functools

import jax
import jax.numpy as jnp
from jax import lax
from jax.experimental import pallas as pl
from jax.experimental.pallas import tpu as pltpu

F32 = jnp.float32
BF16 = jnp.bfloat16

EPS = 1e-6
LANES = 128
SUBLANES = 8
VMEM_LIMIT_BYTES = 56 * 1024 * 1024

SB_HEAD_DIM = 64
SB_TILE = 256
SB_PAIRS = 8
LOG2_E = 1.4426950408889634
EXP2_UNDERFLOW = 150.0
EXP2_CLAMP = 126.0

SSD_HEAD_DIM = 64
SSD_GROUPS = 2
SSD_STATE = 128
SSD_CONV_K = 4
SSD_CHUNK = 128

XA_HEADS = 4


def _params(*sem):
    return pltpu.CompilerParams(dimension_semantics=sem, vmem_limit_bytes=VMEM_LIMIT_BYTES)


def _rms(x):
    return x * lax.rsqrt(jnp.mean(x * x, axis=-1, keepdims=True) + EPS)


def _dot(a, b):
    return jnp.dot(a, b, preferred_element_type=F32)


def _dot_nt(a, b):
    return lax.dot_general(a, b, (((1,), (1,)), ((), ())), preferred_element_type=F32)


def _silu(x):
    return x * jax.nn.sigmoid(x)


def _split_bf16(x, parts):
    out = []
    r = x
    for _ in range(parts - 1):
        p = r.astype(BF16)
        out.append(p)
        r = r - p.astype(F32)
    out.append(r.astype(BF16))
    return out


def _dot_split(x, w_bf16, parts, lhs=True):
    acc = None
    for p in _split_bf16(x, parts):
        t = _dot(p, w_bf16) if lhs else _dot(w_bf16, p)
        acc = t if acc is None else acc + t
    return acc


def _ffn_body(h_ref, pre_ref, wgu_ref, wd_ref, post_ref, o_ref):
    d_ff = wd_ref.shape[0]
    h = h_ref[...]
    hn = (_rms(h) * pre_ref[...]).astype(BF16)
    g = _dot(hn, wgu_ref[:, :d_ff])
    u = _dot(hn, wgu_ref[:, d_ff:])
    f = _dot((_silu(g) * u).astype(BF16), wd_ref[...])
    o_ref[...] = h + 0.5 * (_rms(f) * post_ref[...])


def _ffn(h, pre_g, w_gu, w_down, post_g, *, tm=512):
    t, d = h.shape
    d_ff = w_down.shape[0]
    return pl.pallas_call(
        _ffn_body,
        grid=(t // tm,),
        in_specs=[
            pl.BlockSpec((tm, d), lambda i: (i, 0)),
            pl.BlockSpec((1, d), lambda i: (0, 0)),
            pl.BlockSpec((d, 2 * d_ff), lambda i: (0, 0)),
            pl.BlockSpec((d_ff, d), lambda i: (0, 0)),
            pl.BlockSpec((1, d), lambda i: (0, 0)),
        ],
        out_specs=pl.BlockSpec((tm, d), lambda i: (i, 0)),
        out_shape=jax.ShapeDtypeStruct((t, d), F32),
        compiler_params=_params("parallel"),
        name="ffn",
    )(h, pre_g.reshape(1, d), w_gu, w_down, post_g.reshape(1, d))


def _inproj_body(aw, h_ref, g_ref, w_ref, q_ref, k_ref, v_ref, hn_ref):
    hn = (_rms(h_ref[...]) * g_ref[...]).astype(BF16)
    hn_ref[...] = hn
    scale = LOG2_E / (SB_HEAD_DIM ** 0.5)
    q_ref[...] = (_dot(hn, w_ref[:, 0:aw]) * scale).astype(BF16)
    k_ref[...] = _dot(hn, w_ref[:, aw:2 * aw]).astype(BF16)
    v_ref[...] = _dot(hn, w_ref[:, 2 * aw:3 * aw]).astype(BF16)


def _inproj(h, g, w_qkv, aw, *, tm=512):
    t, d = h.shape
    outs = [aw, aw, aw, d]
    return pl.pallas_call(
        functools.partial(_inproj_body, aw),
        grid=(t // tm,),
        in_specs=[
            pl.BlockSpec((tm, d), lambda i: (i, 0)),
            pl.BlockSpec((1, d), lambda i: (0, 0)),
            pl.BlockSpec((d, 3 * aw), lambda i: (0, 0)),
        ],
        out_specs=[pl.BlockSpec((tm, n), lambda i: (i, 0)) for n in outs],
        out_shape=[jax.ShapeDtypeStruct((t, n), BF16) for n in outs],
        compiler_params=_params("parallel"),
        name="inproj",
    )(h, g.reshape(1, d), w_qkv)


def _attn_body(q_ref, k_ref, v_ref, o_ref, acc_ref, c_ref):
    tt = SB_TILE
    qi = pl.program_id(2)
    lane = lax.broadcasted_iota(jnp.int32, (tt, LANES), 1)
    first = lane < SB_HEAD_DIM
    q_pairs = []
    for pr in range(SB_PAIRS):
        q = q_ref[0, :, pr * LANES:(pr + 1) * LANES]
        zero = jnp.zeros_like(q)
        q_pairs.append(jnp.concatenate([jnp.where(first, q, zero), jnp.where(first, zero, q)], axis=0))
    row = lax.broadcasted_iota(jnp.int32, (2 * tt, tt), 0)
    col = lax.broadcasted_iota(jnp.int32, (2 * tt, tt), 1)
    causal = col < jnp.where(row >= tt, row - tt, row)
    urow = lax.broadcasted_iota(jnp.int32, (tt, tt), 0)
    ucol = lax.broadcasted_iota(jnp.int32, (tt, tt), 1)
    upper = jnp.where(urow > ucol, 1.0, 0.0).astype(BF16)

    def block(pr, kb, c_in, masked):
        start = pl.multiple_of(kb * tt, tt)
        sl = pl.ds(pr * LANES, LANES)
        kblk = k_ref[0, pl.ds(start, tt), sl]
        vblk = v_ref[0, pl.ds(start, tt), sl]
        zero = jnp.zeros_like(vblk)
        v_bd = jnp.concatenate([jnp.where(first, vblk, zero), jnp.where(first, zero, vblk)], axis=0)
        z = jnp.minimum(_dot_nt(q_pairs[pr], kblk), EXP2_CLAMP)
        lp = jnp.log2(1.0 + jnp.exp2(z))
        if masked:
            lp = jnp.where(causal, lp, 0.0)
        e = z - lp - _dot(lp.astype(BF16), upper)
        if c_in is not None:
            e = e - c_in
        a = jnp.exp2(e)
        if masked:
            a = jnp.where(causal, a, 0.0)
        a = a.astype(BF16)
        a_cat = jnp.concatenate([a[:tt], a[tt:]], axis=1)
        return _dot(a_cat, v_bd), jnp.sum(lp, axis=1, keepdims=True)

    kprev = jnp.maximum(qi - 1, 0)
    penalty = jnp.where(qi > 0, 0.0, 1e30).astype(F32)
    mn = None
    for pr in range(SB_PAIRS):
        o_d, tot_d = block(pr, qi, None, True)
        c1 = tot_d + penalty
        o_p, tot_p = block(pr, kprev, c1, False)
        acc_ref[pr] = o_d + o_p
        c2 = c1 + tot_p
        c_ref[pr] = c2
        m = jnp.min(c2)
        mn = m if mn is None else jnp.minimum(mn, m)

    def cond(s):
        kb, mn = s
        return jnp.logical_and(kb >= 0, mn < EXP2_UNDERFLOW)

    def body(s):
        kb, _ = s
        mn = None
        for pr in range(SB_PAIRS):
            c = c_ref[pr]
            o, tot = block(pr, kb, c, False)
            acc_ref[pr] += o
            c = c + tot
            c_ref[pr] = c
            m = jnp.min(c)
            mn = m if mn is None else jnp.minimum(mn, m)
        return kb - 1, mn

    lax.while_loop(cond, body, (qi - 2, mn))
    o_ref[0] = jnp.concatenate([acc_ref[pr] for pr in range(SB_PAIRS)], axis=1).astype(o_ref.dtype)


def _attention(q, k, v):
    b, s, w = q.shape
    tt = SB_TILE
    bw = SB_PAIRS * LANES
    return pl.pallas_call(
        _attn_body,
        grid=(b, w // bw, s // tt),
        in_specs=[
            pl.BlockSpec((1, tt, bw), lambda bi, hp, qi: (bi, qi, hp)),
            pl.BlockSpec((1, s, bw), lambda bi, hp, qi: (bi, 0, hp)),
            pl.BlockSpec((1, s, bw), lambda bi, hp, qi: (bi, 0, hp)),
        ],
        out_specs=pl.BlockSpec((1, tt, bw), lambda bi, hp, qi: (bi, qi, hp)),
        out_shape=jax.ShapeDtypeStruct((b, s, w), BF16),
        scratch_shapes=[pltpu.VMEM((SB_PAIRS, tt, LANES), F32),
                        pltpu.VMEM((SB_PAIRS, 2 * tt, 1), F32)],
        compiler_params=_params("parallel", "parallel", "arbitrary"),
        name="sb_attn",
    )(q, k, v)


def _ssd_body(sw, hn_ref, hn_next_ref, w_ref, cw_ref, cb_ref, dtb_ref, alog_ref, dskip_ref, ng_ref,
              o_ref, xpad_ref, st_ref, pa_ref, pb_ref):
    ll = SSD_CHUNK
    gn = SSD_GROUPS * SSD_STATE
    gw = sw // SSD_GROUPS
    pad = SUBLANES
    ch = xpad_ref.shape[1]
    c = pl.program_id(1)
    row = lax.broadcasted_iota(jnp.int32, (ll, ll), 0)
    col = lax.broadcasted_iota(jnp.int32, (ll, ll), 1)
    tri = col <= row
    tri_b = jnp.where(tri, 1.0, 0.0).astype(BF16)
    er = lax.broadcasted_iota(jnp.int32, (LANES, sw), 0)
    ec = lax.broadcasted_iota(jnp.int32, (LANES, sw), 1)
    expand = jnp.where(ec // SSD_HEAD_DIM == er, 1.0, 0.0).astype(BF16)
    lane = lax.broadcasted_iota(jnp.int32, (ll, LANES), 1)

    @pl.when(c == 0)
    def _():
        xpad_ref[0:pad, :] = jnp.zeros((pad, ch), F32)
        st_ref[...] = jnp.zeros_like(st_ref)
        pa_ref[...] = _dot(hn_ref[0, 0:ll, :], w_ref[...])

    def chunk(p_ref, half):
        xpad_ref[pad:pad + ll, :] = p_ref[:, sw:sw + ch]
        xall = xpad_ref[...]
        acc = cw_ref[0:1, :] * xall
        for k in range(1, SSD_CONV_K):
            acc = pltpu.roll(acc, 1, axis=0) + cw_ref[k:k + 1, :] * xall
        conv = acc[pad:pad + ll, :] + cb_ref[...]
        xpad_ref[0:pad, :] = xpad_ref[ll:ll + pad, :]
        xbc = _silu(conv)
        xs = xbc[:, 0:sw]
        bm = xbc[:, sw:sw + gn]
        cm = xbc[:, sw + gn:sw + 2 * gn]

        dtr = p_ref[:, sw + ch:] + dtb_ref[...]
        dt = jnp.maximum(dtr, 0.0) + jnp.log1p(jnp.exp(-jnp.abs(dtr)))
        adt = dt * (-jnp.exp(alog_ref[...]))
        a_cs = _dot_split(adt, tri_b, 3, lhs=False)
        a_cs_t = a_cs.T
        a_last = a_cs[ll - 1:ll, :]
        fac = jnp.concatenate([dt, jnp.exp(a_last - a_cs)], axis=0)
        fac_e = _dot(fac.astype(BF16), expand)
        dt_e, dec_e = fac_e[0:ll], fac_e[ll:2 * ll]
        ea_e = _dot_split(jnp.exp(a_cs), expand, 2)

        xdt = xs * dt_e
        xdt_b = xdt.astype(BF16)
        xdec_b = (xdt * dec_e).astype(BF16)
        heads_per_group = gw // SSD_HEAD_DIM
        ys = []
        for g in range(SSD_GROUPS):
            bg_t = bm[:, g * SSD_STATE:(g + 1) * SSD_STATE].T.astype(BF16)
            cg = cm[:, g * SSD_STATE:(g + 1) * SSD_STATE].astype(BF16)
            cb = _dot(cg, bg_t)
            pairs = []
            for p in range(heads_per_group // 2):
                res = []
                for e in range(2):
                    hd = g * heads_per_group + 2 * p + e
                    diff = a_cs[:, hd:hd + 1] - a_cs_t[hd:hd + 1, :]
                    lm = jnp.exp(jnp.where(tri, diff, -jnp.inf))
                    blk = (g * heads_per_group + 2 * p) * SSD_HEAD_DIM
                    res.append(_dot((cb * lm).astype(BF16), xdt_b[:, blk:blk + LANES]))
                pairs.append(jnp.where(lane < SSD_HEAD_DIM, res[0], res[1]))
            y_diag = jnp.concatenate(pairs, axis=1)
            sl = slice(g * gw, (g + 1) * gw)
            st = st_ref[g]
            y_off = _dot(cg, st.astype(BF16)) * ea_e[:, sl]
            st_ref[g] = st * ea_e[ll - 1:ll, sl] + _dot(bg_t, xdec_b[:, sl])
            ys.append(y_diag + y_off)
        y = jnp.concatenate(ys, axis=1) + dskip_ref[...] * xs
        y = y * _silu(p_ref[:, 0:sw])
        outs = [_rms(y[:, g * gw:(g + 1) * gw]) for g in range(SSD_GROUPS)]
        o_ref[0, half * ll:(half + 1) * ll, :] = (
            jnp.concatenate(outs, axis=1) * ng_ref[...]).astype(o_ref.dtype)

    pb_ref[...] = _dot(hn_ref[0, ll:2 * ll, :], w_ref[...])
    chunk(pa_ref, 0)
    pa_ref[...] = _dot(hn_next_ref[0], w_ref[...])
    chunk(pb_ref, 1)


def _ssd(hn, w_ssd, conv_w, conv_b, dt_bias, a_log, d_skip, norm_g):
    b, s, d = hn.shape
    ch = conv_w.shape[1]
    nh = dt_bias.shape[0]
    sw = nh * SSD_HEAD_DIM
    ll = SSD_CHUNK

    def head_row(v):
        return jnp.zeros((1, LANES), F32).at[0, :nh].set(v)

    row_spec = lambda n: pl.BlockSpec((1, n), lambda bi, ci: (0, 0))
    return pl.pallas_call(
        functools.partial(_ssd_body, sw),
        grid=(b, s // (2 * ll)),
        in_specs=[
            pl.BlockSpec((1, 2 * ll, d), lambda bi, ci: (bi, ci, 0)),
            pl.BlockSpec((1, ll, d), lambda bi, ci: (bi, jnp.minimum(2 * ci + 2, s // ll - 1), 0)),
            pl.BlockSpec(w_ssd.shape, lambda bi, ci: (0, 0)),
            pl.BlockSpec((SSD_CONV_K, ch), lambda bi, ci: (0, 0)),
            row_spec(ch), row_spec(LANES), row_spec(LANES), row_spec(sw), row_spec(sw),
        ],
        out_specs=pl.BlockSpec((1, 2 * ll, sw), lambda bi, ci: (bi, ci, 0)),
        out_shape=jax.ShapeDtypeStruct((b, s, sw), BF16),
        scratch_shapes=[pltpu.VMEM((ll + SUBLANES, ch), F32),
                        pltpu.VMEM((SSD_GROUPS, SSD_STATE, sw // SSD_GROUPS), F32),
                        pltpu.VMEM((ll, w_ssd.shape[1]), F32),
                        pltpu.VMEM((ll, w_ssd.shape[1]), F32)],
        compiler_params=_params("parallel", "arbitrary"),
        name="ssd",
    )(hn, hn, w_ssd, conv_w, conv_b.reshape(1, ch), head_row(dt_bias), head_row(a_log),
      jnp.repeat(d_skip, SSD_HEAD_DIM).reshape(1, sw), norm_g.reshape(1, sw))


def _outproj_body(oa_ref, os_ref, h_ref, ag_ref, wa_ref, ws_ref, pg_ref, o_ref):
    oan = (_rms(oa_ref[...].astype(F32)) * ag_ref[...]).astype(BF16)
    m = _dot(oan, wa_ref[...]) + _dot(os_ref[...], ws_ref[...])
    o_ref[...] = h_ref[...] + _rms(m) * pg_ref[...]


def _outproj(o_att, o_ssd, h, attn_g, w_out, post_g, *, tm=1024):
    t, d = h.shape
    aw, sw = o_att.shape[1], o_ssd.shape[1]
    assert aw == sw
    return pl.pallas_call(
        _outproj_body,
        grid=(t // tm,),
        in_specs=[
            pl.BlockSpec((tm, aw), lambda i: (i, 0)),
            pl.BlockSpec((tm, sw), lambda i: (i, 0)),
            pl.BlockSpec((tm, d), lambda i: (i, 0)),
            pl.BlockSpec((1, aw), lambda i: (0, 0)),
            pl.BlockSpec((aw, d), lambda i: (0, 0)),
            pl.BlockSpec((sw, d), lambda i: (1, 0)),
            pl.BlockSpec((1, d), lambda i: (0, 0)),
        ],
        out_specs=pl.BlockSpec((tm, d), lambda i: (i, 0)),
        out_shape=jax.ShapeDtypeStruct((t, d), F32),
        compiler_params=_params("parallel"),
        name="outproj",
    )(o_att, o_ssd, h, attn_g.reshape(1, aw), w_out, w_out, post_g.reshape(1, d))


def _memkv_body(mem_ref, g_ref, w_ref, kv_ref):
    kv_ref[0] = _dot((_rms(mem_ref[0]) * g_ref[...]).astype(BF16), w_ref[...]).astype(BF16)


def _memkv(mem, g, wkv):
    b, m, d = mem.shape
    return pl.pallas_call(
        _memkv_body,
        grid=(b,),
        in_specs=[
            pl.BlockSpec((1, m, d), lambda bi: (bi, 0, 0)),
            pl.BlockSpec((1, d), lambda bi: (0, 0)),
            pl.BlockSpec((d, 2 * d), lambda bi: (0, 0)),
        ],
        out_specs=pl.BlockSpec((1, m, 2 * d), lambda bi: (bi, 0, 0)),
        out_shape=jax.ShapeDtypeStruct((b, m, 2 * d), BF16),
        compiler_params=_params("parallel"),
        name="memkv",
    )(mem, g.reshape(1, d), wkv)


def _xattn_body(h_ref, kv_ref, g_ref, wq_ref, wo_ref, pg_ref, o_ref):
    d = h_ref.shape[2]
    hd = d // XA_HEADS
    h = h_ref[0]
    hn = (_rms(h) * g_ref[...]).astype(BF16)
    q = (_dot(hn, wq_ref[...]) * (1.0 / hd ** 0.5)).astype(BF16)
    outs = []
    for i in range(XA_HEADS):
        sc = _dot_nt(q[:, i * hd:(i + 1) * hd], kv_ref[0, :, i * hd:(i + 1) * hd])
        e = jnp.exp(sc - jnp.max(sc, axis=-1, keepdims=True))
        p = e / jnp.sum(e, axis=-1, keepdims=True)
        outs.append(_dot(p.astype(BF16), kv_ref[0, :, d + i * hd:d + (i + 1) * hd]).astype(BF16))
    c = _dot(jnp.concatenate(outs, axis=1), wo_ref[...])
    o_ref[0] = h + _rms(c) * pg_ref[...]


def _xattn(h, kv, pre_g, wq, wo, post_g, *, tm=1024):
    b, s, d = h.shape
    m = kv.shape[1]
    return pl.pallas_call(
        _xattn_body,
        grid=(b, s // tm),
        in_specs=[
            pl.BlockSpec((1, tm, d), lambda bi, i: (bi, i, 0)),
            pl.BlockSpec((1, m, 2 * d), lambda bi, i: (bi, 0, 0)),
            pl.BlockSpec((1, d), lambda bi, i: (0, 0)),
            pl.BlockSpec((d, d), lambda bi, i: (0, 0)),
            pl.BlockSpec((d, d), lambda bi, i: (0, 0)),
            pl.BlockSpec((1, d), lambda bi, i: (0, 0)),
        ],
        out_specs=pl.BlockSpec((1, tm, d), lambda bi, i: (bi, i, 0)),
        out_shape=jax.ShapeDtypeStruct((b, s, d), F32),
        compiler_params=_params("parallel", "parallel"),
        name="xattn",
    )(h, kv, pre_g.reshape(1, d), wq, wo, post_g.reshape(1, d))


def _mixer(h, b, s, mix_pre_g, w_in, conv_w, conv_b, dt_bias, a_log, d_skip, ssd_norm_g,
           attn_norm_g, w_out, mix_post_g):
    d = h.shape[1]
    nh = dt_bias.shape[0]
    sw = nh * SSD_HEAD_DIM
    cw = conv_w.shape[1]
    aw = (w_in.shape[1] - nh - sw - cw) // 3
    w_bf = w_in.astype(BF16)
    w_ssd = jnp.concatenate([w_bf[:, 3 * aw:], jnp.zeros((d, LANES - nh), BF16)], axis=1)
    q, k, v, hn = _inproj(h, mix_pre_g, w_bf[:, :3 * aw], aw)
    o_att = _attention(q.reshape(b, s, aw), k.reshape(b, s, aw), v.reshape(b, s, aw))
    o_ssd = _ssd(hn.reshape(b, s, d), w_ssd, conv_w, conv_b, dt_bias, a_log, d_skip, ssd_norm_g)
    return _outproj(o_att.reshape(b * s, aw), o_ssd.reshape(b * s, sw), h, attn_norm_g,
                    w_out.astype(BF16), mix_post_g)


def kernel(x, mem, ffn1_pre_g, ffn1_w_gu, ffn1_w_down, ffn1_post_g, mix_pre_g, w_in, conv_w, conv_b, dt_bias, a_log, d_skip, ssd_norm_g, attn_norm_g, w_out, mix_post_g, xa_pre_g, mem_g, xa_wq, xa_wkv, xa_wo, xa_post_g, ffn2_pre_g, ffn2_w_gu, ffn2_w_down, ffn2_post_g):
    b, s, d = x.shape
    h = x.reshape(b * s, d)
    for l in range(w_in.shape[0]):
        h = _ffn(h, ffn1_pre_g[l], ffn1_w_gu[l].astype(BF16), ffn1_w_down[l].astype(BF16),
                 ffn1_post_g[l])
        h = _mixer(h, b, s, mix_pre_g[l], w_in[l], conv_w[l], conv_b[l], dt_bias[l], a_log[l],
                   d_skip[l], ssd_norm_g[l], attn_norm_g[l], w_out[l], mix_post_g[l])
        kv = _memkv(mem, mem_g[l], xa_wkv[l].astype(BF16))
        h = _xattn(h.reshape(b, s, d), kv, xa_pre_g[l], xa_wq[l].astype(BF16),
                   xa_wo[l].astype(BF16), xa_post_g[l]).reshape(b * s, d)
        h = _ffn(h, ffn2_pre_g[l], ffn2_w_gu[l].astype(BF16), ffn2_w_down[l].astype(BF16),
                 ffn2_post_g[l])
    return h.reshape(b, s, d)
```

```python
import functools

import jax
import jax.numpy as jnp
from jax import lax
from jax.experimental import pallas as pl
from jax.experimental.pallas import tpu as pltpu

F32 = jnp.float32
BF16 = jnp.bfloat16

EPS = 1e-6
LANES = 128
SUBLANES = 8
VMEM_LIMIT_BYTES = 56 * 1024 * 1024

SB_HEAD_DIM = 64
SB_TILE = 256
SB_PAIRS = 8
LOG2_E = 1.4426950408889634
EXP2_UNDERFLOW = 150.0
EXP2_CLAMP = 126.0

SSD_HEAD_DIM = 64
SSD_GROUPS = 2
SSD_STATE = 128
SSD_CONV_K = 4
SSD_CHUNK = 128

XA_HEADS = 4


def _params(*sem):
    return pltpu.CompilerParams(dimension_semantics=sem, vmem_limit_bytes=VMEM_LIMIT_BYTES)


def _rms(x):
    return x * lax.rsqrt(jnp.mean(x * x, axis=-1, keepdims=True) + EPS)


def _dot(a, b):
    return jnp.dot(a, b, preferred_element_type=F32)


def _dot_nt(a, b):
    return lax.dot_general(a, b, (((1,), (1,)), ((), ())), preferred_element_type=F32)


def _silu(x):
    return x * jax.nn.sigmoid(x)


def _split_bf16(x, parts):
    out = []
    r = x
    for _ in range(parts - 1):
        p = r.astype(BF16)
        out.append(p)
        r = r - p.astype(F32)
    out.append(r.astype(BF16))
    return out


def _dot_split(x, w_bf16, parts, lhs=True):
    acc = None
    for p in _split_bf16(x, parts):
        t = _dot(p, w_bf16) if lhs else _dot(w_bf16, p)
        acc = t if acc is None else acc + t
    return acc


def _ffn_body(split, h_ref, pre_ref, wgu_ref, wd_ref, post_ref, o_ref):
    d_ff = wd_ref.shape[0]
    half = h_ref.shape[0] // split
    for r in range(split):
        rows = slice(r * half, (r + 1) * half)
        h = h_ref[rows, :]
        hn = (_rms(h) * pre_ref[...]).astype(BF16)
        g = _dot(hn, wgu_ref[:, :d_ff])
        u = _dot(hn, wgu_ref[:, d_ff:])
        f = _dot((_silu(g) * u).astype(BF16), wd_ref[...])
        o_ref[rows, :] = h + 0.5 * (_rms(f) * post_ref[...])


def _ffn(h, pre_g, w_gu, w_down, post_g, *, tm=1024, split=4):
    t, d = h.shape
    d_ff = w_down.shape[0]
    return pl.pallas_call(
        functools.partial(_ffn_body, split),
        grid=(t // tm,),
        in_specs=[
            pl.BlockSpec((tm, d), lambda i: (i, 0)),
            pl.BlockSpec((1, d), lambda i: (0, 0)),
            pl.BlockSpec((d, 2 * d_ff), lambda i: (0, 0)),
            pl.BlockSpec((d_ff, d), lambda i: (0, 0)),
            pl.BlockSpec((1, d), lambda i: (0, 0)),
        ],
        out_specs=pl.BlockSpec((tm, d), lambda i: (i, 0)),
        out_shape=jax.ShapeDtypeStruct((t, d), F32),
        compiler_params=_params("parallel"),
        name="ffn",
    )(h, pre_g.reshape(1, d), w_gu, w_down, post_g.reshape(1, d))


def _inproj_body(aw, split, h_ref, g_ref, w_ref, q_ref, k_ref, v_ref, hn_ref):
    scale = LOG2_E / (SB_HEAD_DIM ** 0.5)
    half = h_ref.shape[0] // split
    for r in range(split):
        rows = slice(r * half, (r + 1) * half)
        hn = (_rms(h_ref[rows, :]) * g_ref[...]).astype(BF16)
        hn_ref[rows, :] = hn
        q_ref[rows, :] = (_dot(hn, w_ref[:, 0:aw]) * scale).astype(BF16)
        k_ref[rows, :] = _dot(hn, w_ref[:, aw:2 * aw]).astype(BF16)
        v_ref[rows, :] = _dot(hn, w_ref[:, 2 * aw:3 * aw]).astype(BF16)


def _inproj(h, g, w_qkv, aw, *, tm=1024, split=2):
    t, d = h.shape
    outs = [aw, aw, aw, d]
    return pl.pallas_call(
        functools.partial(_inproj_body, aw, split),
        grid=(t // tm,),
        in_specs=[
            pl.BlockSpec((tm, d), lambda i: (i, 0)),
            pl.BlockSpec((1, d), lambda i: (0, 0)),
            pl.BlockSpec((d, 3 * aw), lambda i: (0, 0)),
        ],
        out_specs=[pl.BlockSpec((tm, n), lambda i: (i, 0)) for n in outs],
        out_shape=[jax.ShapeDtypeStruct((t, n), BF16) for n in outs],
        compiler_params=_params("parallel"),
        name="inproj",
    )(h, g.reshape(1, d), w_qkv)


def _attn_body(q_ref, k_ref, v_ref, o_ref, acc_ref, c_ref):
    tt = SB_TILE
    qi = pl.program_id(2)
    lane = lax.broadcasted_iota(jnp.int32, (tt, LANES), 1)
    first = lane < SB_HEAD_DIM
    q_pairs = []
    for pr in range(SB_PAIRS):
        q = q_ref[0, :, pr * LANES:(pr + 1) * LANES]
        zero = jnp.zeros_like(q)
        q_pairs.append(jnp.concatenate([jnp.where(first, q, zero), jnp.where(first, zero, q)], axis=0))
    row = lax.broadcasted_iota(jnp.int32, (2 * tt, tt), 0)
    col = lax.broadcasted_iota(jnp.int32, (2 * tt, tt), 1)
    causal = col < jnp.where(row >= tt, row - tt, row)
    urow = lax.broadcasted_iota(jnp.int32, (tt, tt), 0)
    ucol = lax.broadcasted_iota(jnp.int32, (tt, tt), 1)
    upper = jnp.where(urow > ucol, 1.0, 0.0).astype(BF16)

    def block(pr, kb, c_in, masked):
        start = pl.multiple_of(kb * tt, tt)
        sl = pl.ds(pr * LANES, LANES)
        kblk = k_ref[0, pl.ds(start, tt), sl]
        vblk = v_ref[0, pl.ds(start, tt), sl]
        zero = jnp.zeros_like(vblk)
        v_bd = jnp.concatenate([jnp.where(first, vblk, zero), jnp.where(first, zero, vblk)], axis=0)
        z = jnp.minimum(_dot_nt(q_pairs[pr], kblk), EXP2_CLAMP)
        lp = jnp.log2(1.0 + jnp.exp2(z))
        if masked:
            lp = jnp.where(causal, lp, 0.0)
        e = z - lp - _dot(lp.astype(BF16), upper)
        if c_in is not None:
            e = e - c_in
        a = jnp.exp2(e)
        if masked:
            a = jnp.where(causal, a, 0.0)
        a = a.astype(BF16)
        a_cat = jnp.concatenate([a[:tt], a[tt:]], axis=1)
        return _dot(a_cat, v_bd), jnp.sum(lp, axis=1, keepdims=True)

    kprev = jnp.maximum(qi - 1, 0)
    penalty = jnp.where(qi > 0, 0.0, 1e30).astype(F32)
    mn = None
    for pr in range(SB_PAIRS):
        o_d, tot_d = block(pr, qi, None, True)
        c1 = tot_d + penalty
        o_p, tot_p = block(pr, kprev, c1, False)
        acc_ref[pr] = o_d + o_p
        c2 = c1 + tot_p
        c_ref[pr] = c2
        m = jnp.min(c2)
        mn = m if mn is None else jnp.minimum(mn, m)

    def cond(s):
        kb, mn = s
        return jnp.logical_and(kb >= 0, mn < EXP2_UNDERFLOW)

    def body(s):
        kb, _ = s
        mn = None
        for pr in range(SB_PAIRS):
            c = c_ref[pr]
            o, tot = block(pr, kb, c, False)
            acc_ref[pr] += o
            c = c + tot
            c_ref[pr] = c
            m = jnp.min(c)
            mn = m if mn is None else jnp.minimum(mn, m)
        return kb - 1, mn

    lax.while_loop(cond, body, (qi - 2, mn))
    o_ref[0] = jnp.concatenate([acc_ref[pr] for pr in range(SB_PAIRS)], axis=1).astype(o_ref.dtype)


def _attention(q, k, v):
    b, s, w = q.shape
    tt = SB_TILE
    bw = SB_PAIRS * LANES
    return pl.pallas_call(
        _attn_body,
        grid=(b, w // bw, s // tt),
        in_specs=[
            pl.BlockSpec((1, tt, bw), lambda bi, hp, qi: (bi, qi, hp)),
            pl.BlockSpec((1, s, bw), lambda bi, hp, qi: (bi, 0, hp)),
            pl.BlockSpec((1, s, bw), lambda bi, hp, qi: (bi, 0, hp)),
        ],
        out_specs=pl.BlockSpec((1, tt, bw), lambda bi, hp, qi: (bi, qi, hp)),
        out_shape=jax.ShapeDtypeStruct((b, s, w), BF16),
        scratch_shapes=[pltpu.VMEM((SB_PAIRS, tt, LANES), F32),
                        pltpu.VMEM((SB_PAIRS, 2 * tt, 1), F32)],
        compiler_params=_params("parallel", "parallel", "arbitrary"),
        name="sb_attn",
    )(q, k, v)


def _ssd_body(sw, hn_ref, hn_next_ref, w_ref, cw_ref, cb_ref, dtb_ref, alog_ref, dskip_ref, ng_ref,
              o_ref, xpad_ref, st_ref, pa_ref, pb_ref):
    ll = SSD_CHUNK
    gn = SSD_GROUPS * SSD_STATE
    gw = sw // SSD_GROUPS
    pad = SUBLANES
    ch = xpad_ref.shape[1]
    c = pl.program_id(1)
    row = lax.broadcasted_iota(jnp.int32, (ll, ll), 0)
    col = lax.broadcasted_iota(jnp.int32, (ll, ll), 1)
    tri = col <= row
    tri_b = jnp.where(tri, 1.0, 0.0).astype(BF16)
    er = lax.broadcasted_iota(jnp.int32, (LANES, sw), 0)
    ec = lax.broadcasted_iota(jnp.int32, (LANES, sw), 1)
    expand = jnp.where(ec // SSD_HEAD_DIM == er, 1.0, 0.0).astype(BF16)
    lane = lax.broadcasted_iota(jnp.int32, (ll, LANES), 1)

    @pl.when(c == 0)
    def _():
        xpad_ref[0:pad, :] = jnp.zeros((pad, ch), F32)
        st_ref[...] = jnp.zeros_like(st_ref)
        pa_ref[...] = _dot(hn_ref[0, 0:ll, :], w_ref[...])

    def chunk(p_ref, half):
        xpad_ref[pad:pad + ll, :] = p_ref[:, sw:sw + ch]
        xall = xpad_ref[...]
        acc = cw_ref[0:1, :] * xall
        for k in range(1, SSD_CONV_K):
            acc = pltpu.roll(acc, 1, axis=0) + cw_ref[k:k + 1, :] * xall
        conv = acc[pad:pad + ll, :] + cb_ref[...]
        xpad_ref[0:pad, :] = xpad_ref[ll:ll + pad, :]
        xbc = _silu(conv)
        xs = xbc[:, 0:sw]
        bm = xbc[:, sw:sw + gn]
        cm = xbc[:, sw + gn:sw + 2 * gn]

        dtr = p_ref[:, sw + ch:] + dtb_ref[...]
        dt = jnp.maximum(dtr, 0.0) + jnp.log1p(jnp.exp(-jnp.abs(dtr)))
        adt = dt * (-jnp.exp(alog_ref[...]))
        a_cs = _dot_split(adt, tri_b, 3, lhs=False)
        a_cs_t = a_cs.T
        a_last = a_cs[ll - 1:ll, :]
        fac = jnp.concatenate([dt, jnp.exp(a_last - a_cs)], axis=0)
        fac_e = _dot(fac.astype(BF16), expand)
        dt_e, dec_e = fac_e[0:ll], fac_e[ll:2 * ll]
        ea_e = _dot_split(jnp.exp(a_cs), expand, 2)

        xdt = xs * dt_e
        xdt_b = xdt.astype(BF16)
        xdec_b = (xdt * dec_e).astype(BF16)
        heads_per_group = gw // SSD_HEAD_DIM
        ys = []
        for g in range(SSD_GROUPS):
            bg_t = bm[:, g * SSD_STATE:(g + 1) * SSD_STATE].T.astype(BF16)
            cg = cm[:, g * SSD_STATE:(g + 1) * SSD_STATE].astype(BF16)
            cb = _dot(cg, bg_t)
            pairs = []
            for p in range(heads_per_group // 2):
                res = []
                for e in range(2):
                    hd = g * heads_per_group + 2 * p + e
                    diff = a_cs[:, hd:hd + 1] - a_cs_t[hd:hd + 1, :]
                    lm = jnp.exp(jnp.where(tri, diff, -jnp.inf))
                    blk = (g * heads_per_group + 2 * p) * SSD_HEAD_DIM
                    res.append(_dot((cb * lm).astype(BF16), xdt_b[:, blk:blk + LANES]))
                pairs.append(jnp.where(lane < SSD_HEAD_DIM, res[0], res[1]))
            y_diag = jnp.concatenate(pairs, axis=1)
            sl = slice(g * gw, (g + 1) * gw)
            st = st_ref[g]
            y_off = _dot(cg, st.astype(BF16)) * ea_e[:, sl]
            st_ref[g] = st * ea_e[ll - 1:ll, sl] + _dot(bg_t, xdec_b[:, sl])
            ys.append(y_diag + y_off)
        y = jnp.concatenate(ys, axis=1) + dskip_ref[...] * xs
        y = y * _silu(p_ref[:, 0:sw])
        outs = [_rms(y[:, g * gw:(g + 1) * gw]) for g in range(SSD_GROUPS)]
        o_ref[0, half * ll:(half + 1) * ll, :] = (
            jnp.concatenate(outs, axis=1) * ng_ref[...]).astype(o_ref.dtype)

    pb_ref[...] = _dot(hn_ref[0, ll:2 * ll, :], w_ref[...])
    chunk(pa_ref, 0)
    pa_ref[...] = _dot(hn_next_ref[0], w_ref[...])
    chunk(pb_ref, 1)


def _ssd(hn, w_ssd, conv_w, conv_b, dt_bias, a_log, d_skip, norm_g):
    b, s, d = hn.shape
    ch = conv_w.shape[1]
    nh = dt_bias.shape[0]
    sw = nh * SSD_HEAD_DIM
    ll = SSD_CHUNK

    def head_row(v):
        return jnp.zeros((1, LANES), F32).at[0, :nh].set(v)

    row_spec = lambda n: pl.BlockSpec((1, n), lambda bi, ci: (0, 0))
    return pl.pallas_call(
        functools.partial(_ssd_body, sw),
        grid=(b, s // (2 * ll)),
        in_specs=[
            pl.BlockSpec((1, 2 * ll, d), lambda bi, ci: (bi, ci, 0)),
            pl.BlockSpec((1, ll, d), lambda bi, ci: (bi, jnp.minimum(2 * ci + 2, s // ll - 1), 0)),
            pl.BlockSpec(w_ssd.shape, lambda bi, ci: (0, 0)),
            pl.BlockSpec((SSD_CONV_K, ch), lambda bi, ci: (0, 0)),
            row_spec(ch), row_spec(LANES), row_spec(LANES), row_spec(sw), row_spec(sw),
        ],
        out_specs=pl.BlockSpec((1, 2 * ll, sw), lambda bi, ci: (bi, ci, 0)),
        out_shape=jax.ShapeDtypeStruct((b, s, sw), BF16),
        scratch_shapes=[pltpu.VMEM((ll + SUBLANES, ch), F32),
                        pltpu.VMEM((SSD_GROUPS, SSD_STATE, sw // SSD_GROUPS), F32),
                        pltpu.VMEM((ll, w_ssd.shape[1]), F32),
                        pltpu.VMEM((ll, w_ssd.shape[1]), F32)],
        compiler_params=_params("parallel", "arbitrary"),
        name="ssd",
    )(hn, hn, w_ssd, conv_w, conv_b.reshape(1, ch), head_row(dt_bias), head_row(a_log),
      jnp.repeat(d_skip, SSD_HEAD_DIM).reshape(1, sw), norm_g.reshape(1, sw))


def _outproj_body(split, oa_ref, os_ref, h_ref, ag_ref, wa_ref, ws_ref, pg_ref, o_ref):
    half = h_ref.shape[0] // split
    for r in range(split):
        rows = slice(r * half, (r + 1) * half)
        oan = (_rms(oa_ref[rows, :].astype(F32)) * ag_ref[...]).astype(BF16)
        m = _dot(oan, wa_ref[...]) + _dot(os_ref[rows, :], ws_ref[...])
        o_ref[rows, :] = h_ref[rows, :] + _rms(m) * pg_ref[...]


def _outproj(o_att, o_ssd, h, attn_g, w_out, post_g, *, tm=1024, split=2):
    t, d = h.shape
    aw, sw = o_att.shape[1], o_ssd.shape[1]
    assert aw == sw
    return pl.pallas_call(
        functools.partial(_outproj_body, split),
        grid=(t // tm,),
        in_specs=[
            pl.BlockSpec((tm, aw), lambda i: (i, 0)),
            pl.BlockSpec((tm, sw), lambda i: (i, 0)),
            pl.BlockSpec((tm, d), lambda i: (i, 0)),
            pl.BlockSpec((1, aw), lambda i: (0, 0)),
            pl.BlockSpec((aw, d), lambda i: (0, 0)),
            pl.BlockSpec((sw, d), lambda i: (1, 0)),
            pl.BlockSpec((1, d), lambda i: (0, 0)),
        ],
        out_specs=pl.BlockSpec((tm, d), lambda i: (i, 0)),
        out_shape=jax.ShapeDtypeStruct((t, d), F32),
        compiler_params=_params("parallel"),
        name="outproj",
    )(o_att, o_ssd, h, attn_g.reshape(1, aw), w_out, w_out, post_g.reshape(1, d))


def _memkv_body(mem_ref, g_ref, w_ref, kv_ref):
    kv_ref[0] = _dot((_rms(mem_ref[0]) * g_ref[...]).astype(BF16), w_ref[...]).astype(BF16)


def _memkv(mem, g, wkv):
    b, m, d = mem.shape
    return pl.pallas_call(
        _memkv_body,
        grid=(b,),
        in_specs=[
            pl.BlockSpec((1, m, d), lambda bi: (bi, 0, 0)),
            pl.BlockSpec((1, d), lambda bi: (0, 0)),
            pl.BlockSpec((d, 2 * d), lambda bi: (0, 0)),
        ],
        out_specs=pl.BlockSpec((1, m, 2 * d), lambda bi: (bi, 0, 0)),
        out_shape=jax.ShapeDtypeStruct((b, m, 2 * d), BF16),
        compiler_params=_params("parallel"),
        name="memkv",
    )(mem, g.reshape(1, d), wkv)


def _xattn_body(split, h_ref, kv_ref, g_ref, wq_ref, wo_ref, pg_ref, o_ref):
    d = h_ref.shape[2]
    hd = d // XA_HEADS
    half = h_ref.shape[1] // split
    for r in range(split):
        rows = slice(r * half, (r + 1) * half)
        h = h_ref[0, rows, :]
        hn = (_rms(h) * g_ref[...]).astype(BF16)
        q = (_dot(hn, wq_ref[...]) * (1.0 / hd ** 0.5)).astype(BF16)
        outs = []
        for i in range(XA_HEADS):
            sc = _dot_nt(q[:, i * hd:(i + 1) * hd], kv_ref[0, :, i * hd:(i + 1) * hd])
            e = jnp.exp(sc - jnp.max(sc, axis=-1, keepdims=True))
            p = e / jnp.sum(e, axis=-1, keepdims=True)
            outs.append(_dot(p.astype(BF16), kv_ref[0, :, d + i * hd:d + (i + 1) * hd]).astype(BF16))
        c = _dot(jnp.concatenate(outs, axis=1), wo_ref[...])
        o_ref[0, rows, :] = h + _rms(c) * pg_ref[...]


def _xattn(h, kv, pre_g, wq, wo, post_g, *, tm=1024, split=2):
    b, s, d = h.shape
    m = kv.shape[1]
    return pl.pallas_call(
        functools.partial(_xattn_body, split),
        grid=(b, s // tm),
        in_specs=[
            pl.BlockSpec((1, tm, d), lambda bi, i: (bi, i, 0)),
            pl.BlockSpec((1, m, 2 * d), lambda bi, i: (bi, 0, 0)),
            pl.BlockSpec((1, d), lambda bi, i: (0, 0)),
            pl.BlockSpec((d, d), lambda bi, i: (0, 0)),
            pl.BlockSpec((d, d), lambda bi, i: (0, 0)),
            pl.BlockSpec((1, d), lambda bi, i: (0, 0)),
        ],
        out_specs=pl.BlockSpec((1, tm, d), lambda bi, i: (bi, i, 0)),
        out_shape=jax.ShapeDtypeStruct((b, s, d), F32),
        compiler_params=_params("parallel", "parallel"),
        name="xattn",
    )(h, kv, pre_g.reshape(1, d), wq, wo, post_g.reshape(1, d))


def _mixer(h, b, s, mix_pre_g, w_in, conv_w, conv_b, dt_bias, a_log, d_skip, ssd_norm_g,
           attn_norm_g, w_out, mix_post_g):
    d = h.shape[1]
    nh = dt_bias.shape[0]
    sw = nh * SSD_HEAD_DIM
    cw = conv_w.shape[1]
    aw = (w_in.shape[1] - nh - sw - cw) // 3
    w_bf = w_in.astype(BF16)
    w_ssd = jnp.concatenate([w_bf[:, 3 * aw:], jnp.zeros((d, LANES - nh), BF16)], axis=1)
    q, k, v, hn = _inproj(h, mix_pre_g, w_bf[:, :3 * aw], aw)
    o_att = _attention(q.reshape(b, s, aw), k.reshape(b, s, aw), v.reshape(b, s, aw))
    o_ssd = _ssd(hn.reshape(b, s, d), w_ssd, conv_w, conv_b, dt_bias, a_log, d_skip, ssd_norm_g)
    return _outproj(o_att.reshape(b * s, aw), o_ssd.reshape(b * s, sw), h, attn_norm_g,
                    w_out.astype(BF16), mix_post_g)


def kernel(x, mem, ffn1_pre_g, ffn1_w_gu, ffn1_w_down, ffn1_post_g, mix_pre_g, w_in, conv_w, conv_b, dt_bias, a_log, d_skip, ssd_norm_g, attn_norm_g, w_out, mix_post_g, xa_pre_g, mem_g, xa_wq, xa_wkv, xa_wo, xa_post_g, ffn2_pre_g, ffn2_w_gu, ffn2_w_down, ffn2_post_g):
    b, s, d = x.shape
    h = x.reshape(b * s, d)
    for l in range(w_in.shape[0]):
        h = _ffn(h, ffn1_pre_g[l], ffn1_w_gu[l].astype(BF16), ffn1_w_down[l].astype(BF16),
                 ffn1_post_g[l])
        h = _mixer(h, b, s, mix_pre_g[l], w_in[l], conv_w[l], conv_b[l], dt_bias[l], a_log[l],
                   d_skip[l], ssd_norm_g[l], attn_norm_g[l], w_out[l], mix_post_g[l])
        kv = _memkv(mem, mem_g[l], xa_wkv[l].astype(BF16))
        h = _xattn(h.reshape(b, s, d), kv, xa_pre_g[l], xa_wq[l].astype(BF16),
                   xa_wo[l].astype(BF16), xa_post_g[l]).reshape(b * s, d)
        h = _ffn(h, ffn2_pre_g[l], ffn2_w_gu[l].astype(BF16), ffn2_w_down[l].astype(BF16),
                 ffn2_post_g[l])
    return h.reshape(b, s, d)
```
